```python
import jax
import jax.numpy as jnp
from jax import lax
import numpy as np

D_MODEL = 4096
BATCH = 8
SEQ = 2048
DEPTH = 2

HEAD_DIM = 128
N_HEADS_TOTAL = D_MODEL // HEAD_DIM
MLA_HEADS = N_HEADS_TOTAL // 4
MLA_Q_LORA = D_MODEL // 4
MLA_KV_LORA = D_MODEL // 8
MLA_NOPE_DIM = 128
MLA_ROPE_DIM = 64
MLA_V_DIM = 128
MLA_QK_DIM = MLA_NOPE_DIM + MLA_ROPE_DIM
ROPE_THETA = 10000.0
SWA_HEADS = N_HEADS_TOTAL // 2
SWA_KV_HEADS = max(1, SWA_HEADS // 8)
WINDOW = 128
FOX_HEADS = N_HEADS_TOTAL // 4
Q_BLOCK = 128
N_BRANCH = 3
FFN_HIDDEN = ((8 * D_MODEL // 3 + 255) // 256) * 256
RMS_EPS = 1e-6
MAX_POS_OFFSET = 1024

IN_SPLITS = (
    MLA_Q_LORA,
    MLA_KV_LORA,
    MLA_ROPE_DIM,
    SWA_HEADS * HEAD_DIM,
    SWA_KV_HEADS * HEAD_DIM,
    SWA_KV_HEADS * HEAD_DIM,
    FOX_HEADS * HEAD_DIM,
    FOX_HEADS * HEAD_DIM,
    FOX_HEADS * HEAD_DIM,
    FOX_HEADS,
    N_BRANCH * D_MODEL,
)
IN_WIDTH = sum(IN_SPLITS)
MLA_OUT = MLA_HEADS * MLA_V_DIM
SWA_OUT = SWA_HEADS * HEAD_DIM
FOX_OUT = FOX_HEADS * HEAD_DIM
MIX_WIDTH = MLA_OUT + SWA_OUT + FOX_OUT

kernel_name = "hybrid_mla_swa_fox_gated_block"


def rms_norm(x, gain):
    xf = x.astype(jnp.float32)
    y = xf * lax.rsqrt(jnp.mean(xf * xf, axis=-1, keepdims=True) + RMS_EPS)
    return (y * gain.astype(jnp.float32)).astype(x.dtype)


def split_points():
    pts, acc = [], 0
    for s in IN_SPLITS[:-1]:
        acc += s
        pts.append(acc)
    return pts


def apply_rope(x, positions):
    half = x.shape[-1] // 2
    inv_freq = ROPE_THETA ** (-jnp.arange(half, dtype=jnp.float32) / half)
    ang = positions.astype(jnp.float32)[..., None] * inv_freq
    cos = jnp.cos(ang)[:, :, None, :]
    sin = jnp.sin(ang)[:, :, None, :]
    xf = x.astype(jnp.float32)
    x1, x2 = xf[..., :half], xf[..., half:]
    out = jnp.concatenate([x1 * cos - x2 * sin, x2 * cos + x1 * sin], axis=-1)
    return out.astype(x.dtype)


def alibi_slopes(n_heads):
    return 2.0 ** (-8.0 * jnp.arange(1, n_heads + 1, dtype=jnp.float32) / n_heads)


def causal_block_attention(q, k, v, scale, log_decay=None):
    B, S, H, _ = q.shape
    nb = S // Q_BLOCK
    qb = jnp.moveaxis(q.reshape(B, nb, Q_BLOCK, H, -1), 1, 0)
    key_idx = jnp.arange(S)

    def attend(blk, qi, fq):
        s = jnp.einsum("bqhd,bshd->bhqs", qi, k).astype(jnp.float32) * scale
        if fq is not None:
            s = s + jnp.transpose(fq, (0, 2, 1))[..., None] - fk
        q_idx = blk * Q_BLOCK + jnp.arange(Q_BLOCK)
        s = jnp.where(key_idx[None, :] <= q_idx[:, None], s, -jnp.inf)
        p = jax.nn.softmax(s, axis=-1)
        return jnp.einsum("bhqs,bshd->bqhd", p.astype(v.dtype), v)

    blocks = jnp.arange(nb)
    if log_decay is None:
        out = lax.map(lambda a: attend(a[0], a[1], None), (blocks, qb))
    else:
        ld = log_decay.astype(jnp.float32)
        fk = jnp.transpose(ld, (0, 2, 1))[:, :, None, :]
        fqb = jnp.moveaxis(ld.reshape(B, nb, Q_BLOCK, H), 1, 0)
        out = lax.map(lambda a: attend(a[0], a[1], a[2]), (blocks, qb, fqb))
    return jnp.moveaxis(out, 0, 1).reshape(B, S, -1)


def sliding_window_attention(q, k, v, positions, sinks, slopes):
    B, S, H, Dh = q.shape
    KVH = k.shape[2]
    G = H // KVH
    nb = S // WINDOW
    qb = q.reshape(B, nb, WINDOW, KVH, G, Dh)

    def with_prev(t):
        tb = t.reshape((B, nb, WINDOW) + t.shape[2:])
        prev = jnp.concatenate([jnp.zeros_like(tb[:, :1]), tb[:, :-1]], axis=1)
        return jnp.concatenate([prev, tb], axis=2)

    kw, vw, pw = with_prev(k), with_prev(v), with_prev(positions)
    qpos = positions.reshape(B, nb, WINDOW)
    s = jnp.einsum("bnqkgd,bnskd->bnkgqs", qb, kw).astype(jnp.float32) * (Dh ** -0.5)
    i = jnp.arange(WINDOW)[:, None]
    j = jnp.arange(2 * WINDOW)[None, :]
    d_idx = i + WINDOW - j
    blk = jnp.arange(nb)[:, None, None]
    valid = (d_idx >= 0) & (d_idx < WINDOW) & ((blk > 0) | (j >= WINDOW))
    dist = (qpos[:, :, :, None] - pw[:, :, None, :]).astype(jnp.float32)
    m_h = slopes.reshape(KVH, G)[None, None, :, :, None, None]
    s = s - m_h * dist[:, :, None, None]
    s = jnp.where(valid[None, :, None, None], s, -jnp.inf)
    sink = sinks.astype(jnp.float32).reshape(KVH, G)[None, None, :, :, None, None]
    mx = jnp.maximum(jnp.max(s, axis=-1, keepdims=True), sink)
    e = jnp.exp(s - mx)
    p = e / (jnp.sum(e, axis=-1, keepdims=True) + jnp.exp(sink - mx))
    o = jnp.einsum("bnkgqs,bnskd->bnqkgd", p.astype(v.dtype), vw)
    return o.reshape(B, S, H * Dh)


def hybrid_mixer(h, positions, w_in, g_q_lora, w_uq, g_kv_lora, w_ukv, b_forget, swa_sinks, w_branch, w_out):
    B, S, _ = h.shape
    proj = h @ w_in
    (cq, ckv, kr, q_swa, k_swa, v_swa, q_fox, k_fox, v_fox, z_fox, gate_logits) = jnp.split(
        proj, split_points(), axis=-1)

    q = (rms_norm(cq, g_q_lora) @ w_uq).reshape(B, S, MLA_HEADS, MLA_QK_DIM)
    q = jnp.concatenate([q[..., :MLA_NOPE_DIM], apply_rope(q[..., MLA_NOPE_DIM:], positions)], axis=-1)
    kv = (rms_norm(ckv, g_kv_lora) @ w_ukv).reshape(B, S, MLA_HEADS, MLA_NOPE_DIM + MLA_V_DIM)
    k_nope, v = kv[..., :MLA_NOPE_DIM], kv[..., MLA_NOPE_DIM:]
    k_rope = apply_rope(kr[:, :, None, :], positions)
    k = jnp.concatenate([k_nope, jnp.broadcast_to(k_rope, (B, S, MLA_HEADS, MLA_ROPE_DIM))], axis=-1)
    o_mla = causal_block_attention(q, k, v, MLA_QK_DIM ** -0.5)

    o_swa = sliding_window_attention(
        q_swa.reshape(B, S, SWA_HEADS, HEAD_DIM),
        k_swa.reshape(B, S, SWA_KV_HEADS, HEAD_DIM),
        v_swa.reshape(B, S, SWA_KV_HEADS, HEAD_DIM),
        positions, swa_sinks, alibi_slopes(SWA_HEADS))

    log_f = jax.nn.log_sigmoid((z_fox + b_forget).astype(jnp.float32))
    cum_log_f = lax.cumsum(log_f, axis=1)
    o_fox = causal_block_attention(
        q_fox.reshape(B, S, FOX_HEADS, HEAD_DIM),
        k_fox.reshape(B, S, FOX_HEADS, HEAD_DIM),
        v_fox.reshape(B, S, FOX_HEADS, HEAD_DIM),
        HEAD_DIM ** -0.5, log_decay=cum_log_f)

    gates = jax.nn.sigmoid(gate_logits).reshape(B, S, N_BRANCH, D_MODEL)
    r1 = MLA_OUT
    r2 = MLA_OUT + SWA_OUT
    merged = (gates[:, :, 0] * (o_mla @ w_branch[:r1])
              + gates[:, :, 1] * (o_swa @ w_branch[r1:r2])
              + gates[:, :, 2] * (o_fox @ w_branch[r2:]))
    return merged @ w_out


def setup_inputs(seed: int = 0) -> dict:
    key = jax.random.key(seed)
    ks = jax.random.split(key, 20)
    L, D = DEPTH, D_MODEL

    def normal(k, shape, scale):
        return jax.random.normal(k, shape, jnp.float32) * scale

    def gain(k, n):
        return 1.0 + 0.05 * jax.random.normal(k, (L, n), jnp.float32)

    x = normal(ks[0], (BATCH, SEQ, D), 1.0)
    offsets = jax.random.randint(ks[1], (BATCH, 1), 0, MAX_POS_OFFSET, dtype=jnp.int32)
    positions = offsets + jnp.arange(SEQ, dtype=jnp.int32)[None, :]
    return {
        "x": x,
        "positions": positions,
        "g_mix_pre": gain(ks[2], D),
        "g_mix_post": gain(ks[3], D),
        "g_ffn_pre": gain(ks[4], D),
        "g_ffn_post": gain(ks[5], D),
        "w_in": normal(ks[6], (L, D, IN_WIDTH), D ** -0.5),
        "g_q_lora": gain(ks[7], MLA_Q_LORA),
        "w_uq": normal(ks[8], (L, MLA_Q_LORA, MLA_HEADS * MLA_QK_DIM), MLA_Q_LORA ** -0.5),
        "g_kv_lora": gain(ks[9], MLA_KV_LORA),
        "w_ukv": normal(ks[10], (L, MLA_KV_LORA, MLA_HEADS * (MLA_NOPE_DIM + MLA_V_DIM)), MLA_KV_LORA ** -0.5),
        "b_forget": 2.0 + 0.5 * jax.random.normal(ks[11], (L, FOX_HEADS), jnp.float32),
        "swa_sinks": normal(ks[12], (L, SWA_HEADS), 0.5),
        "w_branch": normal(ks[13], (L, MIX_WIDTH, D), MIX_WIDTH ** -0.5),
        "w_out": normal(ks[14], (L, D, D), D ** -0.5),
        "w_gate_up": normal(ks[15], (L, D, 2 * FFN_HIDDEN), D ** -0.5),
        "w_down": normal(ks[16], (L, FFN_HIDDEN, D), FFN_HIDDEN ** -0.5),
    }


def reference(x, positions, g_mix_pre, g_mix_post, g_ffn_pre, g_ffn_post, w_in, g_q_lora, w_uq,
              g_kv_lora, w_ukv, b_forget, swa_sinks, w_branch, w_out, w_gate_up, w_down):
    for l in range(DEPTH):
        h = rms_norm(x, g_mix_pre[l])
        m = hybrid_mixer(h, positions, w_in[l], g_q_lora[l], w_uq[l], g_kv_lora[l], w_ukv[l],
                         b_forget[l], swa_sinks[l], w_branch[l], w_out[l])
        x = x + rms_norm(m, g_mix_post[l])
        h = rms_norm(x, g_ffn_pre[l])
        gate, up = jnp.split(h @ w_gate_up[l], 2, axis=-1)
        x = x + rms_norm((jax.nn.silu(gate) * up) @ w_down[l], g_ffn_post[l])
    return x
```

```python
import functools

import jax
import jax.numpy as jnp
from jax import lax
from jax.experimental import pallas as pl
from jax.experimental.pallas import tpu as pltpu

F32 = jnp.float32
BF16 = jnp.bfloat16

VMEM_LIMIT_BYTES = 56 * 1024 * 1024
LANES = 128

HEAD_DIM = 128
MLA_HEADS = 8
MLA_Q_LORA = 1024
MLA_KV_LORA = 512
MLA_NOPE = 128
MLA_ROPE = 64
MLA_QK = MLA_NOPE + MLA_ROPE
SWA_HEADS = 16
SWA_KV_HEADS = 2
SWA_GROUP = SWA_HEADS // SWA_KV_HEADS
WINDOW = 128
FOX_HEADS = 8
ROPE_THETA = 10000.0
RMS_EPS = 1e-6
Q_TILE = 256


def _params(*sem):
    return pltpu.CompilerParams(dimension_semantics=sem, vmem_limit_bytes=VMEM_LIMIT_BYTES)


def _rms(x, gain):
    return x * lax.rsqrt(jnp.mean(x * x, axis=-1, keepdims=True) + RMS_EPS) * gain


def _dot(a, b):
    return jnp.dot(a, b, preferred_element_type=F32)


def _dot_nt(a, b):
    return lax.dot_general(a, b, (((1,), (1,)), ((), ())), preferred_element_type=F32)


def _sigmoid(x):
    return 1.0 / (1.0 + jnp.exp(-x))


def _rope_table_kernel(pos_ref, c_ref, cos_ref, sin_ref):
    ang = pos_ref[...].astype(F32) * c_ref[0:1, :]
    cos_ref[...] = jnp.cos(ang) * c_ref[1:2, :]
    sin_ref[...] = jnp.sin(ang) * c_ref[2:3, :]


def rope_tables(pos_b, consts, tm=2048):
    m = pos_b.shape[0]
    spec = pl.BlockSpec((tm, LANES), lambda i: (i, 0))
    return pl.pallas_call(
        _rope_table_kernel,
        grid=(m // tm,),
        in_specs=[spec, pl.BlockSpec((8, LANES), lambda i: (0, 0))],
        out_specs=[spec, spec],
        out_shape=[jax.ShapeDtypeStruct((m, LANES), F32)] * 2,
        compiler_params=_params("parallel"),
        name="rope_tables",
    )(pos_b, consts)


def _prenorm_kernel(x_ref, g_ref, h_ref):
    h_ref[...] = _rms(x_ref[...], g_ref[...]).astype(h_ref.dtype)


def prenorm(x, gain, tm=256):
    m, d = x.shape
    return pl.pallas_call(
        _prenorm_kernel,
        grid=(m // tm,),
        in_specs=[pl.BlockSpec((tm, d), lambda i: (i, 0)), pl.BlockSpec((1, d), lambda i: (0, 0))],
        out_specs=pl.BlockSpec((tm, d), lambda i: (i, 0)),
        out_shape=jax.ShapeDtypeStruct((m, d), BF16),
        compiler_params=_params("parallel"),
        name="prenorm",
    )(x, gain)


def _norm_residual_kernel(y_ref, x_ref, gp_ref, gn_ref, xo_ref, h_ref):
    xn = x_ref[...] + _rms(y_ref[...], gp_ref[...])
    xo_ref[...] = xn
    h_ref[...] = _rms(xn, gn_ref[...]).astype(h_ref.dtype)


def _norm_residual_last_kernel(y_ref, x_ref, gp_ref, xo_ref):
    xo_ref[...] = x_ref[...] + _rms(y_ref[...], gp_ref[...])


def norm_residual(y, x, g_post, g_next, tm=256):
    m, d = x.shape
    row = pl.BlockSpec((tm, d), lambda i: (i, 0))
    vec = pl.BlockSpec((1, d), lambda i: (0, 0))
    if g_next is None:
        return pl.pallas_call(
            _norm_residual_last_kernel,
            grid=(m // tm,),
            in_specs=[row, row, vec],
            out_specs=row,
            out_shape=jax.ShapeDtypeStruct((m, d), F32),
            compiler_params=_params("parallel"),
            name="norm_residual_last",
        )(y, x, g_post), None
    return pl.pallas_call(
        _norm_residual_kernel,
        grid=(m // tm,),
        in_specs=[row, row, vec, vec],
        out_specs=[row, row],
        out_shape=[jax.ShapeDtypeStruct((m, d), F32), jax.ShapeDtypeStruct((m, d), BF16)],
        compiler_params=_params("parallel"),
        name="norm_residual",
    )(y, x, g_post, g_next)


def _matmul_kernel(a_ref, w_ref, o_ref, *, act):
    acc = _dot(a_ref[...], w_ref[...])
    if act == "sigmoid":
        acc = _sigmoid(acc)
    o_ref[...] = acc.astype(o_ref.dtype)


def matmul(a, w, out_dtype, tm, tn, act=None, name="matmul"):
    m, k = a.shape
    n = w.shape[1]
    assert m % tm == 0 and n % tn == 0
    return pl.pallas_call(
        functools.partial(_matmul_kernel, act=act),
        grid=(m // tm, n // tn),
        in_specs=[pl.BlockSpec((tm, k), lambda i, j: (i, 0)), pl.BlockSpec((k, tn), lambda i, j: (0, j))],
        out_specs=pl.BlockSpec((tm, tn), lambda i, j: (i, j)),
        out_shape=jax.ShapeDtypeStruct((m, n), out_dtype),
        compiler_params=_params("parallel", "arbitrary"),
        name=name,
    )(a, w)


def _latent_kernel(h_ref, w_ref, gq_ref, gkv_ref, cos_ref, sin_ref, cq_ref, ckv_ref, kr_ref, z_ref):
    acc = _dot(h_ref[...], w_ref[...])
    q_end = MLA_Q_LORA
    kv_end = q_end + MLA_KV_LORA
    cq_ref[...] = _rms(acc[:, :q_end], gq_ref[...]).astype(cq_ref.dtype)
    ckv_ref[...] = _rms(acc[:, q_end:kv_end], gkv_ref[...]).astype(ckv_ref.dtype)
    slab = acc[:, kv_end:kv_end + LANES]
    rot = slab * cos_ref[...] + pltpu.roll(slab, LANES // 2, 1) * sin_ref[...]
    kr_ref[...] = rot.astype(kr_ref.dtype)
    z_ref[...] = acc[:, kv_end + LANES:kv_end + 2 * LANES]


def latent_proj(h, w_lat, g_q, g_kv, cos_t, sin_t, tm=512):
    m, d = h.shape
    n = w_lat.shape[1]
    row = lambda width: pl.BlockSpec((tm, width), lambda i: (i, 0))
    const = lambda r, c: pl.BlockSpec((r, c), lambda i: (0, 0))
    return pl.pallas_call(
        _latent_kernel,
        grid=(m // tm,),
        in_specs=[row(d), const(d, n), const(1, MLA_Q_LORA), const(1, MLA_KV_LORA), row(LANES), row(LANES)],
        out_specs=[row(MLA_Q_LORA), row(MLA_KV_LORA), row(LANES), row(LANES)],
        out_shape=[jax.ShapeDtypeStruct((m, MLA_Q_LORA), BF16), jax.ShapeDtypeStruct((m, MLA_KV_LORA), BF16),
                   jax.ShapeDtypeStruct((m, LANES), BF16), jax.ShapeDtypeStruct((m, LANES), F32)],
        compiler_params=_params("parallel"),
        name="latent_proj",
    )(h, w_lat, g_q, g_kv, cos_t, sin_t)


def _q_up_kernel(a_ref, w_ref, cos_ref, sin_ref, o_ref, *, scale):
    acc = _dot(a_ref[...], w_ref[...])
    c = cos_ref[...]
    s = sin_ref[...]
    for h in range(MLA_HEADS):
        base = 2 * LANES * h
        nope = acc[:, base:base + LANES]
        slab = acc[:, base + LANES:base + 2 * LANES]
        rot = slab * c + pltpu.roll(slab, LANES // 2, 1) * s
        o_ref[:, base:base + LANES] = (nope * scale).astype(o_ref.dtype)
        o_ref[:, base + LANES:base + 2 * LANES] = (rot * scale).astype(o_ref.dtype)


def q_up_proj(cqn, w_uq, cos_t, sin_t, tm=1024):
    m, k = cqn.shape
    n = w_uq.shape[1]
    return pl.pallas_call(
        functools.partial(_q_up_kernel, scale=MLA_QK ** -0.5),
        grid=(m // tm,),
        in_specs=[pl.BlockSpec((tm, k), lambda i: (i, 0)), pl.BlockSpec((k, n), lambda i: (0, 0)),
                  pl.BlockSpec((tm, LANES), lambda i: (i, 0)), pl.BlockSpec((tm, LANES), lambda i: (i, 0))],
        out_specs=pl.BlockSpec((tm, n), lambda i: (i, 0)),
        out_shape=jax.ShapeDtypeStruct((m, n), BF16),
        compiler_params=_params("parallel"),
        name="q_up_proj",
    )(cqn, w_uq, cos_t, sin_t)


def _causal_blocks(q_ref, k_ref, v_ref, o_ref, seq, tq, scale, fq_ref, fk_row):
    row = lax.broadcasted_iota(jnp.int32, (tq, tq), 0)
    col = lax.broadcasted_iota(jnp.int32, (tq, tq), 1)
    tri = col <= row
    for i in range(seq // tq):
        lo, hi = i * tq, (i + 1) * tq
        qi = q_ref[lo:hi, :]
        s_d = _dot_nt(qi, k_ref[lo:hi, :])
        if scale is not None:
            s_d = s_d * scale
        if fq_ref is not None:
            fq = fq_ref[lo:hi, 0:1]
            s_d = s_d + fq - fk_row[:, lo:hi]
        s_d = jnp.where(tri, s_d, -jnp.inf)
        mx = jnp.max(s_d, axis=-1, keepdims=True)
        if i > 0:
            s_o = _dot_nt(qi, k_ref[:lo, :])
            if scale is not None:
                s_o = s_o * scale
            if fq_ref is not None:
                s_o = s_o + fq - fk_row[:, :lo]
            mx = jnp.maximum(mx, jnp.max(s_o, axis=-1, keepdims=True))
        p_d = jnp.exp(s_d - mx)
        den = jnp.sum(p_d, axis=-1, keepdims=True)
        acc = _dot(p_d.astype(BF16), v_ref[lo:hi, :])
        if i > 0:
            p_o = jnp.exp(s_o - mx)
            den = den + jnp.sum(p_o, axis=-1, keepdims=True)
            acc = acc + _dot(p_o.astype(BF16), v_ref[:lo, :])
        o_ref[lo:hi, :] = (acc / den).astype(o_ref.dtype)


def _mla_attn_kernel(q_ref, kn_ref, kr_ref, v_ref, o_ref, k_scr, *, seq, tq):
    k_scr[:, :LANES] = kn_ref[...]
    k_scr[:, LANES:] = kr_ref[...]
    _causal_blocks(q_ref, k_scr, v_ref, o_ref, seq, tq, None, None, None)


def mla_attention(q, kv, k_rot, batch, seq):
    m = q.shape[0]
    return pl.pallas_call(
        functools.partial(_mla_attn_kernel, seq=seq, tq=Q_TILE),
        grid=(batch, MLA_HEADS),
        in_specs=[pl.BlockSpec((seq, 2 * LANES), lambda b, h: (b, h)),
                  pl.BlockSpec((seq, LANES), lambda b, h: (b, 2 * h)),
                  pl.BlockSpec((seq, LANES), lambda b, h: (b, 0)),
                  pl.BlockSpec((seq, LANES), lambda b, h: (b, 2 * h + 1))],
        out_specs=pl.BlockSpec((seq, LANES), lambda b, h: (b, h)),
        out_shape=jax.ShapeDtypeStruct((m, MLA_HEADS * LANES), BF16),
        scratch_shapes=[pltpu.VMEM((seq, 2 * LANES), BF16)],
        compiler_params=_params("parallel", "arbitrary"),
        name="mla_attention",
    )(q, kv, k_rot, kv)


def _fox_attn_kernel(q_ref, k_ref, v_ref, ft_ref, o_ref, fq_scr, *, seq, tq, scale):
    h = pl.program_id(1)
    fk_row = ft_ref[0, pl.ds(h, 1), :]
    fq_scr[...] = jnp.transpose(jnp.broadcast_to(fk_row, (LANES, seq)))
    _causal_blocks(q_ref, k_ref, v_ref, o_ref, seq, tq, scale, fq_scr, fk_row)


def fox_attention(qkv, f_t, batch, seq):
    m = qkv.shape[0]
    nh = FOX_HEADS
    return pl.pallas_call(
        functools.partial(_fox_attn_kernel, seq=seq, tq=Q_TILE, scale=HEAD_DIM ** -0.5),
        grid=(batch, nh),
        in_specs=[pl.BlockSpec((seq, LANES), lambda b, h: (b, h)),
                  pl.BlockSpec((seq, LANES), lambda b, h: (b, nh + h)),
                  pl.BlockSpec((seq, LANES), lambda b, h: (b, 2 * nh + h)),
                  pl.BlockSpec((1, nh, seq), lambda b, h: (b, 0, 0))],
        out_specs=pl.BlockSpec((seq, LANES), lambda b, h: (b, h)),
        out_shape=jax.ShapeDtypeStruct((m, nh * LANES), BF16),
        scratch_shapes=[pltpu.VMEM((seq, LANES), F32)],
        compiler_params=_params("parallel", "arbitrary"),
        name="fox_attention",
    )(qkv, qkv, qkv, f_t)


def _fox_decay_kernel(z_ref, b_ref, ft_ref, *, seq):
    x = z_ref[...] + b_ref[...]
    log_f = jnp.minimum(x, 0.0) - jnp.log1p(jnp.exp(-jnp.abs(x)))
    t = jnp.transpose(log_f)[0:FOX_HEADS, :]
    lane = lax.broadcasted_iota(jnp.int32, t.shape, 1)
    shift = 1
    while shift < seq:
        t = t + jnp.where(lane >= shift, pltpu.roll(t, shift, 1), 0.0)
        shift *= 2
    ft_ref[0] = t


def fox_decay(z_slab, b_row, batch, seq):
    return pl.pallas_call(
        functools.partial(_fox_decay_kernel, seq=seq),
        grid=(batch,),
        in_specs=[pl.BlockSpec((seq, LANES), lambda b: (b, 0)), pl.BlockSpec((1, LANES), lambda b: (0, 0))],
        out_specs=pl.BlockSpec((1, FOX_HEADS, seq), lambda b: (b, 0, 0)),
        out_shape=jax.ShapeDtypeStruct((batch, FOX_HEADS, seq), F32),
        compiler_params=_params("parallel"),
        name="fox_decay",
    )(z_slab, b_row)


def _swa_kernel(slope_ref, sink_ref, q_ref, kp_ref, kc_ref, vp_ref, vc_ref, qpos_ref, pp_ref, pc_ref, o_ref,
                *, scale):
    n = pl.program_id(1)
    w = WINDOW
    i = lax.broadcasted_iota(jnp.int32, (w, w), 0)
    j = lax.broadcasted_iota(jnp.int32, (w, w), 1)
    valid = jnp.concatenate([(j > i) & (n > 0), j <= i], axis=1)
    qpos = qpos_ref[:, 0:1]
    dist = jnp.concatenate([qpos - pp_ref[0], qpos - pc_ref[0]], axis=1).astype(F32)
    for kh in range(SWA_KV_HEADS):
        ks = slice(kh * HEAD_DIM, (kh + 1) * HEAD_DIM)
        k_win = jnp.concatenate([kp_ref[:, ks], kc_ref[:, ks]], axis=0)
        v_win = jnp.concatenate([vp_ref[:, ks], vc_ref[:, ks]], axis=0)
        for g in range(SWA_GROUP):
            hd = kh * SWA_GROUP + g
            hs = slice(hd * HEAD_DIM, (hd + 1) * HEAD_DIM)
            s = _dot_nt(q_ref[:, hs], k_win) * scale - slope_ref[hd] * dist
            s = jnp.where(valid, s, -jnp.inf)
            sink = sink_ref[hd]
            mx = jnp.maximum(jnp.max(s, axis=-1, keepdims=True), sink)
            e = jnp.exp(s - mx)
            den = jnp.sum(e, axis=-1, keepdims=True) + jnp.exp(sink - mx)
            o_ref[:, hs] = _dot((e / den).astype(BF16), v_win).astype(o_ref.dtype)


def swa_attention(qkv, pos_b, pos_row, slopes, sinks, batch, seq):
    m = qkv.shape[0]
    w = WINDOW
    nb = seq // w
    qw = SWA_HEADS * HEAD_DIM
    kvw = SWA_KV_HEADS * HEAD_DIM
    cur = lambda b, n: b * nb + n
    prev = lambda b, n: b * nb + jnp.maximum(n - 1, 0)
    smem = pl.BlockSpec(memory_space=pltpu.SMEM)
    return pl.pallas_call(
        functools.partial(_swa_kernel, scale=HEAD_DIM ** -0.5),
        grid=(batch, nb),
        in_specs=[smem, smem,
                  pl.BlockSpec((w, qw), lambda b, n: (cur(b, n), 0)),
                  pl.BlockSpec((w, kvw), lambda b, n: (prev(b, n), qw // kvw)),
                  pl.BlockSpec((w, kvw), lambda b, n: (cur(b, n), qw // kvw)),
                  pl.BlockSpec((w, kvw), lambda b, n: (prev(b, n), qw // kvw + 1)),
                  pl.BlockSpec((w, kvw), lambda b, n: (cur(b, n), qw // kvw + 1)),
                  pl.BlockSpec((w, LANES), lambda b, n: (cur(b, n), 0)),
                  pl.BlockSpec((1, 1, w), lambda b, n: (prev(b, n), 0, 0)),
                  pl.BlockSpec((1, 1, w), lambda b, n: (cur(b, n), 0, 0))],
        out_specs=pl.BlockSpec((w, qw), lambda b, n: (cur(b, n), 0)),
        out_shape=jax.ShapeDtypeStruct((m, qw), BF16),
        compiler_params=_params("parallel", "arbitrary"),
        name="swa_attention",
    )(slopes, sinks, qkv, qkv, qkv, qkv, qkv, pos_b, pos_row, pos_row)


def _merge_kernel(a0_ref, a1_ref, a2_ref, w0_ref, w1_ref, w2_ref, g0_ref, g1_ref, g2_ref, o_ref):
    acc = g0_ref[...].astype(F32) * _dot(a0_ref[...], w0_ref[...])
    acc = acc + g1_ref[...].astype(F32) * _dot(a1_ref[...], w1_ref[...])
    acc = acc + g2_ref[...].astype(F32) * _dot(a2_ref[...], w2_ref[...])
    o_ref[...] = acc.astype(o_ref.dtype)


def gated_merge(outs, ws, gates, tm=1024, tn=512):
    m = outs[0].shape[0]
    d = ws[0].shape[1]
    nj = d // tn
    a_specs = [pl.BlockSpec((tm, o.shape[1]), lambda i, j: (i, 0)) for o in outs]
    w_specs = [pl.BlockSpec((w.shape[0], tn), lambda i, j: (0, j)) for w in ws]
    g_specs = [pl.BlockSpec((tm, tn), functools.partial(lambda i, j, r: (i, r * nj + j), r=r)) for r in range(3)]
    return pl.pallas_call(
        _merge_kernel,
        grid=(m // tm, nj),
        in_specs=a_specs + w_specs + g_specs,
        out_specs=pl.BlockSpec((tm, tn), lambda i, j: (i, j)),
        out_shape=jax.ShapeDtypeStruct((m, d), BF16),
        compiler_params=_params("parallel", "arbitrary"),
        name="gated_merge",
    )(*outs, *ws, gates, gates, gates)


def _ffn_up_kernel(h_ref, wg_ref, wu_ref, o_ref):
    gate = _dot(h_ref[...], wg_ref[...])
    up = _dot(h_ref[...], wu_ref[...])
    o_ref[...] = (gate * _sigmoid(gate) * up).astype(o_ref.dtype)


def ffn_up(h, w_gate, w_up, tm=2048, tn=256):
    m, d = h.shape
    n = w_gate.shape[1]
    assert n % tn == 0
    return pl.pallas_call(
        _ffn_up_kernel,
        grid=(m // tm, n // tn),
        in_specs=[pl.BlockSpec((tm, d), lambda i, j: (i, 0)),
                  pl.BlockSpec((d, tn), lambda i, j: (0, j)),
                  pl.BlockSpec((d, tn), lambda i, j: (0, j))],
        out_specs=pl.BlockSpec((tm, tn), lambda i, j: (i, j)),
        out_shape=jax.ShapeDtypeStruct((m, n), BF16),
        compiler_params=_params("parallel", "arbitrary"),
        name="ffn_up",
    )(h, w_gate, w_up)


def _layer_weights(l, w_in, w_uq, w_ukv, w_branch, w_out, w_gate_up, w_down, b_forget):
    w = w_in[l]
    d = w.shape[0]
    o = 0
    cuts = {}
    for name, width in (("cq", MLA_Q_LORA), ("ckv", MLA_KV_LORA), ("kr", MLA_ROPE),
                        ("swa", (SWA_HEADS + 2 * SWA_KV_HEADS) * HEAD_DIM), ("fox", 3 * FOX_HEADS * HEAD_DIM),
                        ("z", FOX_HEADS), ("gate", 3 * d)):
        cuts[name] = (o, o + width)
        o += width
    sl = lambda name: w[:, cuts[name][0]:cuts[name][1]]
    half = MLA_ROPE // 2
    kr = sl("kr")
    kr_swapped = jnp.concatenate([kr[:, half:], kr[:, :half]], axis=1)
    z_pad = jnp.zeros((d, LANES - FOX_HEADS), w.dtype)
    w_lat = jnp.concatenate([sl("cq"), sl("ckv"), kr, kr_swapped, sl("z"), z_pad], axis=1).astype(BF16)

    uq = w_uq[l].reshape(MLA_Q_LORA, MLA_HEADS, MLA_QK)
    rope = uq[:, :, MLA_NOPE:]
    rope_swapped = jnp.concatenate([rope[:, :, half:], rope[:, :, :half]], axis=2)
    w_q = jnp.concatenate([uq[:, :, :MLA_NOPE], rope, rope_swapped], axis=2)
    w_q = w_q.reshape(MLA_Q_LORA, MLA_HEADS * 2 * LANES).astype(BF16)

    r1 = MLA_HEADS * HEAD_DIM
    r2 = r1 + SWA_HEADS * HEAD_DIM
    wb = w_branch[l]
    n_ffn = w_down.shape[1]
    b_row = jnp.zeros((1, LANES), F32).at[0, :FOX_HEADS].set(b_forget[l])
    return dict(
        lat=w_lat, uq=w_q, ukv=w_ukv[l].astype(BF16),
        swa=sl("swa").astype(BF16), fox=sl("fox").astype(BF16), gate=sl("gate").astype(BF16),
        branch=(wb[:r1].astype(BF16), wb[r1:r2].astype(BF16), wb[r2:].astype(BF16)),
        out=w_out[l].astype(BF16),
        ffn_gate=w_gate_up[l][:, :n_ffn].astype(BF16), ffn_up=w_gate_up[l][:, n_ffn:].astype(BF16),
        down=w_down[l].astype(BF16), b_row=b_row)


def _rope_consts():
    half = MLA_ROPE // 2
    inv_freq = ROPE_THETA ** (-jnp.arange(half, dtype=F32) / half)
    zeros = jnp.zeros((LANES - MLA_ROPE,), F32)
    ones = jnp.ones((half,), F32)
    rows = jnp.stack([jnp.concatenate([inv_freq, inv_freq, zeros]),
                      jnp.concatenate([ones, ones, zeros]),
                      jnp.concatenate([-ones, ones, zeros])])
    return jnp.concatenate([rows, jnp.zeros((5, LANES), F32)], axis=0)


def kernel(x, positions, g_mix_pre, g_mix_post, g_ffn_pre, g_ffn_post, w_in, g_q_lora, w_uq, g_kv_lora, w_ukv,
           b_forget, swa_sinks, w_branch, w_out, w_gate_up, w_down):
    batch, seq, d = x.shape
    depth = w_in.shape[0]
    m = batch * seq
    xs = x.reshape(m, d)
    pos_b = jnp.broadcast_to(positions.reshape(m, 1), (m, LANES))
    pos_row = positions.reshape(m // WINDOW, 1, WINDOW)
    slopes = 2.0 ** (-8.0 * jnp.arange(1, SWA_HEADS + 1, dtype=F32) / SWA_HEADS)
    cos_t, sin_t = rope_tables(pos_b, _rope_consts())
    vec = lambda g: g.reshape(1, -1)

    h = prenorm(xs, vec(g_mix_pre[0]))
    for l in range(depth):
        w = _layer_weights(l, w_in, w_uq, w_ukv, w_branch, w_out, w_gate_up, w_down, b_forget)
        cqn, ckvn, k_rot, z_slab = latent_proj(h, w["lat"], vec(g_q_lora[l]), vec(g_kv_lora[l]), cos_t, sin_t)
        q_mla = q_up_proj(cqn, w["uq"], cos_t, sin_t)
        kv_mla = matmul(ckvn, w["ukv"], BF16, 1024, 1024, name="kv_up_proj")
        qkv_swa = matmul(h, w["swa"], BF16, 1024, 1280, name="swa_proj")
        qkv_fox = matmul(h, w["fox"], BF16, 1024, 1024, name="fox_proj")
        gates = matmul(h, w["gate"], BF16, 1024, 1024, act="sigmoid", name="gate_proj")
        f_t = fox_decay(z_slab, w["b_row"], batch, seq)
        o_mla = mla_attention(q_mla, kv_mla, k_rot, batch, seq)
        o_swa = swa_attention(qkv_swa, pos_b, pos_row, slopes, swa_sinks[l].astype(F32), batch, seq)
        o_fox = fox_attention(qkv_fox, f_t, batch, seq)
        merged = gated_merge((o_mla, o_swa, o_fox), w["branch"], gates)
        y = matmul(merged, w["out"], F32, 1024, 1024, name="out_proj")
        xs, h2 = norm_residual(y, xs, vec(g_mix_post[l]), vec(g_ffn_pre[l]))
        act = ffn_up(h2, w["ffn_gate"], w["ffn_up"])
        y = matmul(act, w["down"], F32, 512, 512, name="down_proj")
        g_next = vec(g_mix_pre[l + 1]) if l + 1 < depth else None
        xs, h = norm_residual(y, xs, vec(g_ffn_post[l]), g_next)
    return xs.reshape(batch, seq, d)
```

```python
import functools

import jax
import jax.numpy as jnp
from jax import lax
from jax.experimental import pallas as pl
from jax.experimental.pallas import tpu as pltpu

F32 = jnp.float32
BF16 = jnp.bfloat16

VMEM_LIMIT_BYTES = 56 * 1024 * 1024
LANES = 128

HEAD_DIM = 128
MLA_HEADS = 8
MLA_Q_LORA = 1024
MLA_KV_LORA = 512
MLA_NOPE = 128
MLA_ROPE = 64
MLA_QK = MLA_NOPE + MLA_ROPE
SWA_HEADS = 16
SWA_KV_HEADS = 2
SWA_GROUP = SWA_HEADS // SWA_KV_HEADS
WINDOW = 128
FOX_HEADS = 8
ROPE_THETA = 10000.0
RMS_EPS = 1e-6
Q_TILE = 256

SWA_WIDTH = (SWA_HEADS + 2 * SWA_KV_HEADS) * HEAD_DIM
FOX_WIDTH = 3 * FOX_HEADS * HEAD_DIM
LAT_WIDTH = MLA_Q_LORA + MLA_KV_LORA + 2 * LANES


def _params(*sem):
    return pltpu.CompilerParams(dimension_semantics=sem, vmem_limit_bytes=VMEM_LIMIT_BYTES)


def _inv_rms(x):
    return lax.rsqrt(jnp.mean(x * x, axis=-1, keepdims=True) + RMS_EPS)


def _dot(a, b):
    return jnp.dot(a, b, preferred_element_type=F32)


def _dot_nt(a, b):
    return lax.dot_general(a, b, (((1,), (1,)), ((), ())), preferred_element_type=F32)


def _sigmoid(x):
    return 1.0 / (1.0 + jnp.exp(-x))


def _row_chunks(rows, chunks):
    step = rows // chunks
    return [slice(c * step, (c + 1) * step) for c in range(chunks)]


def _rope_table_kernel(pos_ref, c_ref, cos_ref, sin_ref):
    ang = pos_ref[...].astype(F32) * c_ref[0:1, :]
    cos_ref[...] = jnp.cos(ang) * c_ref[1:2, :]
    sin_ref[...] = jnp.sin(ang) * c_ref[2:3, :]


def rope_tables(pos_b, consts, tm=2048):
    m = pos_b.shape[0]
    spec = pl.BlockSpec((tm, LANES), lambda i: (i, 0))
    return pl.pallas_call(
        _rope_table_kernel,
        grid=(m // tm,),
        in_specs=[spec, pl.BlockSpec((8, LANES), lambda i: (0, 0))],
        out_specs=[spec, spec],
        out_shape=[jax.ShapeDtypeStruct((m, LANES), F32)] * 2,
        compiler_params=_params("parallel"),
        name="rope_tables",
    )(pos_b, consts)


def _prenorm_kernel(x_ref, g_ref, h_ref, r_ref):
    x = x_ref[...]
    h_ref[...] = (x * g_ref[...]).astype(h_ref.dtype)
    r_ref[...] = jnp.broadcast_to(_inv_rms(x), r_ref.shape)


def prenorm(x, gain, tm=256):
    m, d = x.shape
    return pl.pallas_call(
        _prenorm_kernel,
        grid=(m // tm,),
        in_specs=[pl.BlockSpec((tm, d), lambda i: (i, 0)), pl.BlockSpec((1, d), lambda i: (0, 0))],
        out_specs=[pl.BlockSpec((tm, d), lambda i: (i, 0)), pl.BlockSpec((tm, LANES), lambda i: (i, 0))],
        out_shape=[jax.ShapeDtypeStruct((m, d), BF16), jax.ShapeDtypeStruct((m, LANES), F32)],
        compiler_params=_params("parallel"),
        name="prenorm",
    )(x, gain)


def _matmul_kernel(*refs, act, scaled, chunks):
    if scaled:
        a_ref, r_ref, w_ref, o_ref = refs
    else:
        a_ref, w_ref, o_ref = refs
    for rs in _row_chunks(a_ref.shape[0], chunks):
        acc = _dot(a_ref[rs, :], w_ref[...])
        if scaled:
            acc = acc * r_ref[rs, 0:1]
        if act == "sigmoid":
            acc = _sigmoid(acc)
        o_ref[rs, :] = acc.astype(o_ref.dtype)


def matmul(a, w, layer, col0, n, out_dtype, tm, tn, row_scale=None, act=None, chunks=1, name="matmul"):
    m, k = a.shape
    assert m % tm == 0 and n % tn == 0 and col0 % tn == 0 and w.shape[1] == k
    col_block0 = col0 // tn
    scaled = row_scale is not None
    in_specs = [pl.BlockSpec((tm, k), lambda i, j: (i, 0))]
    args = [a]
    if scaled:
        in_specs.append(pl.BlockSpec((tm, LANES), lambda i, j: (i, 0)))
        args.append(row_scale)
    in_specs.append(pl.BlockSpec((None, k, tn), lambda i, j: (layer, 0, col_block0 + j)))
    args.append(w)
    return pl.pallas_call(
        functools.partial(_matmul_kernel, act=act, scaled=scaled, chunks=chunks),
        grid=(m // tm, n // tn),
        in_specs=in_specs,
        out_specs=pl.BlockSpec((tm, tn), lambda i, j: (i, j)),
        out_shape=jax.ShapeDtypeStruct((m, n), out_dtype),
        compiler_params=_params("parallel", "arbitrary"),
        name=name,
    )(*args)


def _proj_norm_kernel(*refs, n_i, n_j, d, emit_next):
    if emit_next:
        a_ref, w_ref, x_ref, gp_ref, gn_ref, xo_ref, h_ref, r_ref, y_scr, acc_scr, done_scr, acc2_scr = refs
    else:
        a_ref, w_ref, x_ref, gp_ref, xo_ref, y_scr, acc_scr, done_scr = refs
    i = pl.program_id(0)
    j = pl.program_id(1)

    @pl.when((i == 0) & (j == 0))
    def _():
        y_scr[...] = jnp.zeros_like(y_scr)
        acc_scr[...] = jnp.zeros_like(acc_scr)
        done_scr[...] = jnp.zeros_like(done_scr)
        if emit_next:
            acc2_scr[...] = jnp.zeros_like(acc2_scr)

    first = j == 0
    y_prev = y_scr[j]
    y_new = _dot(a_ref[...], w_ref[...])
    y_scr[j] = y_new
    part = jnp.sum(y_new * y_new, axis=-1, keepdims=True)
    acc_old = acc_scr[...]
    done = jnp.where(first, acc_old, done_scr[...])
    done_scr[...] = done
    acc_scr[...] = jnp.where(first, part, acc_old + part)

    x_new = x_ref[...] + y_prev * lax.rsqrt(done * (1.0 / d) + RMS_EPS) * gp_ref[...]
    xo_ref[...] = x_new
    if emit_next:
        h_ref[...] = (x_new * gn_ref[...]).astype(h_ref.dtype)
        part2 = jnp.sum(x_new * x_new, axis=-1, keepdims=True)
        acc2 = jnp.where(first, part2, acc2_scr[...] + part2)
        acc2_scr[...] = acc2
        r_ref[...] = jnp.broadcast_to(lax.rsqrt(acc2 * (1.0 / d) + RMS_EPS), r_ref.shape)


def proj_norm(a, w, layer, x, g_post, g_next, tm, tn):
    m, k = a.shape
    d = w.shape[2]
    n_i, n_j = m // tm, d // tn
    emit_next = g_next is not None
    prev = lambda i: jnp.maximum(i - 1, 0)
    col = pl.BlockSpec((tm, tn), lambda i, j: (prev(i), j))
    col_out = pl.BlockSpec((tm, tn), lambda i, j: (prev(i), jnp.where(i == 0, 0, j)))
    vec = pl.BlockSpec((1, tn), lambda i, j: (0, j))
    in_specs = [pl.BlockSpec((tm, k), lambda i, j: (jnp.minimum(i, n_i - 1), 0)),
                pl.BlockSpec((None, k, tn), lambda i, j: (layer, 0, j)),
                col, vec]
    args = [a, w, x, g_post]
    out_specs = [col_out]
    out_shape = [jax.ShapeDtypeStruct((m, d), F32)]
    scratch = [pltpu.VMEM((n_j, tm, tn), F32), pltpu.VMEM((tm, 1), F32), pltpu.VMEM((tm, 1), F32)]
    if emit_next:
        in_specs.append(vec)
        args.append(g_next)
        out_specs += [col_out, pl.BlockSpec((tm, LANES), lambda i, j: (prev(i), 0))]
        out_shape += [jax.ShapeDtypeStruct((m, d), BF16), jax.ShapeDtypeStruct((m, LANES), F32)]
        scratch.append(pltpu.VMEM((tm, 1), F32))
    res = pl.pallas_call(
        functools.partial(_proj_norm_kernel, n_i=n_i, n_j=n_j, d=d, emit_next=emit_next),
        grid=(n_i + 1, n_j),
        in_specs=in_specs,
        out_specs=out_specs,
        out_shape=out_shape,
        scratch_shapes=scratch,
        compiler_params=_params("arbitrary", "arbitrary"),
        name="proj_norm" if emit_next else "proj_norm_last",
    )(*args)
    return res if emit_next else (res[0], None, None)


def _latent_kernel(h_ref, r_ref, w_ref, gq_ref, gkv_ref, cos_ref, sin_ref, cq_ref, ckv_ref, kr_ref, z_ref,
                   *, chunks):
    q_end = MLA_Q_LORA
    kv_end = q_end + MLA_KV_LORA
    for rs in _row_chunks(h_ref.shape[0], chunks):
        acc = _dot(h_ref[rs, :], w_ref[...]) * r_ref[rs, 0:1]
        cq = acc[:, :q_end]
        ckv = acc[:, q_end:kv_end]
        cq_ref[rs, :] = (cq * _inv_rms(cq) * gq_ref[...]).astype(cq_ref.dtype)
        ckv_ref[rs, :] = (ckv * _inv_rms(ckv) * gkv_ref[...]).astype(ckv_ref.dtype)
        slab = acc[:, kv_end:kv_end + LANES]
        rot = slab * cos_ref[rs, :] + pltpu.roll(slab, LANES // 2, 1) * sin_ref[rs, :]
        kr_ref[rs, :] = rot.astype(kr_ref.dtype)
        z_ref[rs, :] = acc[:, kv_end + LANES:kv_end + 2 * LANES]


def latent_proj(h, r, w_all, layer, col_block, g_q, g_kv, cos_t, sin_t, tm=512, chunks=2):
    m, d = h.shape
    n = LAT_WIDTH
    row = lambda width: pl.BlockSpec((tm, width), lambda i: (i, 0))
    const = lambda c: pl.BlockSpec((1, c), lambda i: (0, 0))
    return pl.pallas_call(
        functools.partial(_latent_kernel, chunks=chunks),
        grid=(m // tm,),
        in_specs=[row(d), row(LANES), pl.BlockSpec((None, d, n), lambda i: (layer, 0, col_block)),
                  const(MLA_Q_LORA), const(MLA_KV_LORA), row(LANES), row(LANES)],
        out_specs=[row(MLA_Q_LORA), row(MLA_KV_LORA), row(LANES), row(LANES)],
        out_shape=[jax.ShapeDtypeStruct((m, MLA_Q_LORA), BF16), jax.ShapeDtypeStruct((m, MLA_KV_LORA), BF16),
                   jax.ShapeDtypeStruct((m, LANES), BF16), jax.ShapeDtypeStruct((m, LANES), F32)],
        compiler_params=_params("parallel"),
        name="latent_proj",
    )(h, r, w_all, g_q, g_kv, cos_t, sin_t)


def _q_up_kernel(a_ref, w_ref, cos_ref, sin_ref, o_ref, *, scale):
    acc = _dot(a_ref[...], w_ref[...])
    c = cos_ref[...]
    s = sin_ref[...]
    for h in range(MLA_HEADS):
        base = 2 * LANES * h
        nope = acc[:, base:base + LANES]
        slab = acc[:, base + LANES:base + 2 * LANES]
        rot = slab * c + pltpu.roll(slab, LANES // 2, 1) * s
        o_ref[:, base:base + LANES] = (nope * scale).astype(o_ref.dtype)
        o_ref[:, base + LANES:base + 2 * LANES] = (rot * scale).astype(o_ref.dtype)


def q_up_proj(cqn, w_uq, layer, cos_t, sin_t, tm=1024):
    m, k = cqn.shape
    n = w_uq.shape[2]
    return pl.pallas_call(
        functools.partial(_q_up_kernel, scale=MLA_QK ** -0.5),
        grid=(m // tm,),
        in_specs=[pl.BlockSpec((tm, k), lambda i: (i, 0)), pl.BlockSpec((None, k, n), lambda i: (layer, 0, 0)),
                  pl.BlockSpec((tm, LANES), lambda i: (i, 0)), pl.BlockSpec((tm, LANES), lambda i: (i, 0))],
        out_specs=pl.BlockSpec((tm, n), lambda i: (i, 0)),
        out_shape=jax.ShapeDtypeStruct((m, n), BF16),
        compiler_params=_params("parallel"),
        name="q_up_proj",
    )(cqn, w_uq, cos_t, sin_t)


def _causal_blocks(q_ref, k_ref, v_ref, o_ref, seq, tq, scale, fq_ref, fk_row):
    row = lax.broadcasted_iota(jnp.int32, (tq, tq), 0)
    col = lax.broadcasted_iota(jnp.int32, (tq, tq), 1)
    tri = col <= row
    for i in range(seq // tq):
        lo, hi = i * tq, (i + 1) * tq
        qi = q_ref[lo:hi, :]
        s_d = _dot_nt(qi, k_ref[lo:hi, :])
        if scale is not None:
            s_d = s_d * scale
        if fq_ref is not None:
            fq = fq_ref[lo:hi, 0:1]
            s_d = s_d + fq - fk_row[:, lo:hi]
        s_d = jnp.where(tri, s_d, -jnp.inf)
        mx = jnp.max(s_d, axis=-1, keepdims=True)
        if i > 0:
            s_o = _dot_nt(qi, k_ref[:lo, :])
            if scale is not None:
                s_o = s_o * scale
            if fq_ref is not None:
                s_o = s_o + fq - fk_row[:, :lo]
            mx = jnp.maximum(mx, jnp.max(s_o, axis=-1, keepdims=True))
        p_d = jnp.exp(s_d - mx)
        den = jnp.sum(p_d, axis=-1, keepdims=True)
        acc = _dot(p_d.astype(BF16), v_ref[lo:hi, :])
        if i > 0:
            p_o = jnp.exp(s_o - mx)
            den = den + jnp.sum(p_o, axis=-1, keepdims=True)
            acc = acc + _dot(p_o.astype(BF16), v_ref[:lo, :])
        o_ref[lo:hi, :] = (acc / den).astype(o_ref.dtype)


def _mla_attn_kernel(q_ref, kn_ref, kr_ref, v_ref, o_ref, k_scr, *, seq, tq):
    k_scr[:, :LANES] = kn_ref[...]
    k_scr[:, LANES:] = kr_ref[...]
    _causal_blocks(q_ref, k_scr, v_ref, o_ref, seq, tq, None, None, None)


def mla_attention(q, kv, k_rot, batch, seq):
    m = q.shape[0]
    return pl.pallas_call(
        functools.partial(_mla_attn_kernel, seq=seq, tq=Q_TILE),
        grid=(batch, MLA_HEADS),
        in_specs=[pl.BlockSpec((seq, 2 * LANES), lambda b, h: (b, h)),
                  pl.BlockSpec((seq, LANES), lambda b, h: (b, 2 * h)),
                  pl.BlockSpec((seq, LANES), lambda b, h: (b, 0)),
                  pl.BlockSpec((seq, LANES), lambda b, h: (b, 2 * h + 1))],
        out_specs=pl.BlockSpec((seq, LANES), lambda b, h: (b, h)),
        out_shape=jax.ShapeDtypeStruct((m, MLA_HEADS * LANES), BF16),
        scratch_shapes=[pltpu.VMEM((seq, 2 * LANES), BF16)],
        compiler_params=_params("parallel", "arbitrary"),
        name="mla_attention",
    )(q, kv, k_rot, kv)


def _fox_attn_kernel(q_ref, k_ref, v_ref, ft_ref, o_ref, fq_scr, *, seq, tq, scale):
    h = pl.program_id(1)
    fk_row = ft_ref[0, pl.ds(h, 1), :]
    fq_scr[...] = jnp.transpose(jnp.broadcast_to(fk_row, (LANES, seq)))
    _causal_blocks(q_ref, k_ref, v_ref, o_ref, seq, tq, scale, fq_scr, fk_row)


def fox_attention(qkv, f_t, batch, seq):
    m = qkv.shape[0]
    nh = FOX_HEADS
    return pl.pallas_call(
        functools.partial(_fox_attn_kernel, seq=seq, tq=Q_TILE, scale=HEAD_DIM ** -0.5),
        grid=(batch, nh),
        in_specs=[pl.BlockSpec((seq, LANES), lambda b, h: (b, h)),
                  pl.BlockSpec((seq, LANES), lambda b, h: (b, nh + h)),
                  pl.BlockSpec((seq, LANES), lambda b, h: (b, 2 * nh + h)),
                  pl.BlockSpec((1, nh, seq), lambda b, h: (b, 0, 0))],
        out_specs=pl.BlockSpec((seq, LANES), lambda b, h: (b, h)),
        out_shape=jax.ShapeDtypeStruct((m, nh * LANES), BF16),
        scratch_shapes=[pltpu.VMEM((seq, LANES), F32)],
        compiler_params=_params("parallel", "arbitrary"),
        name="fox_attention",
    )(qkv, qkv, qkv, f_t)


def _fox_decay_kernel(z_ref, b_ref, ft_ref, *, seq):
    x = z_ref[...] + b_ref[...]
    log_f = jnp.minimum(x, 0.0) - jnp.log1p(jnp.exp(-jnp.abs(x)))
    t = jnp.transpose(log_f)[0:FOX_HEADS, :]
    lane = lax.broadcasted_iota(jnp.int32, t.shape, 1)
    shift = 1
    while shift < seq:
        t = t + jnp.where(lane >= shift, pltpu.roll(t, shift, 1), 0.0)
        shift *= 2
    ft_ref[0] = t


def fox_decay(z_slab, b_row, batch, seq):
    return pl.pallas_call(
        functools.partial(_fox_decay_kernel, seq=seq),
        grid=(batch,),
        in_specs=[pl.BlockSpec((seq, LANES), lambda b: (b, 0)), pl.BlockSpec((1, LANES), lambda b: (0, 0))],
        out_specs=pl.BlockSpec((1, FOX_HEADS, seq), lambda b: (b, 0, 0)),
        out_shape=jax.ShapeDtypeStruct((batch, FOX_HEADS, seq), F32),
        compiler_params=_params("parallel"),
        name="fox_decay",
    )(z_slab, b_row)


def _swa_kernel(slope_ref, sink_ref, q_ref, kp_ref, kc_ref, vp_ref, vc_ref, qpos_ref, pp_ref, pc_ref, o_ref,
                *, scale):
    n = pl.program_id(1)
    w = WINDOW
    i = lax.broadcasted_iota(jnp.int32, (w, w), 0)
    j = lax.broadcasted_iota(jnp.int32, (w, w), 1)
    valid = jnp.concatenate([(j > i) & (n > 0), j <= i], axis=1)
    qpos = qpos_ref[:, 0:1]
    dist = jnp.concatenate([qpos - pp_ref[0], qpos - pc_ref[0]], axis=1).astype(F32)
    for kh in range(SWA_KV_HEADS):
        ks = slice(kh * HEAD_DIM, (kh + 1) * HEAD_DIM)
        k_win = jnp.concatenate([kp_ref[:, ks], kc_ref[:, ks]], axis=0)
        v_win = jnp.concatenate([vp_ref[:, ks], vc_ref[:, ks]], axis=0)
        for g in range(SWA_GROUP):
            hd = kh * SWA_GROUP + g
            hs = slice(hd * HEAD_DIM, (hd + 1) * HEAD_DIM)
            s = _dot_nt(q_ref[:, hs], k_win) * scale - slope_ref[hd] * dist
            s = jnp.where(valid, s, -jnp.inf)
            sink = sink_ref[hd]
            mx = jnp.maximum(jnp.max(s, axis=-1, keepdims=True), sink)
            e = jnp.exp(s - mx)
            den = jnp.sum(e, axis=-1, keepdims=True) + jnp.exp(sink - mx)
            o_ref[:, hs] = _dot((e / den).astype(BF16), v_win).astype(o_ref.dtype)


def swa_attention(qkv, pos_b, pos_row, slopes, sinks, batch, seq):
    m = qkv.shape[0]
    w = WINDOW
    nb = seq // w
    qw = SWA_HEADS * HEAD_DIM
    kvw = SWA_KV_HEADS * HEAD_DIM
    cur = lambda b, n: b * nb + n
    prev = lambda b, n: b * nb + jnp.maximum(n - 1, 0)
    smem = pl.BlockSpec(memory_space=pltpu.SMEM)
    return pl.pallas_call(
        functools.partial(_swa_kernel, scale=HEAD_DIM ** -0.5),
        grid=(batch, nb),
        in_specs=[smem, smem,
                  pl.BlockSpec((w, qw), lambda b, n: (cur(b, n), 0)),
                  pl.BlockSpec((w, kvw), lambda b, n: (prev(b, n), qw // kvw)),
                  pl.BlockSpec((w, kvw), lambda b, n: (cur(b, n), qw // kvw)),
                  pl.BlockSpec((w, kvw), lambda b, n: (prev(b, n), qw // kvw + 1)),
                  pl.BlockSpec((w, kvw), lambda b, n: (cur(b, n), qw // kvw + 1)),
                  pl.BlockSpec((w, LANES), lambda b, n: (cur(b, n), 0)),
                  pl.BlockSpec((1, 1, w), lambda b, n: (prev(b, n), 0, 0)),
                  pl.BlockSpec((1, 1, w), lambda b, n: (cur(b, n), 0, 0))],
        out_specs=pl.BlockSpec((w, qw), lambda b, n: (cur(b, n), 0)),
        out_shape=jax.ShapeDtypeStruct((m, qw), BF16),
        compiler_params=_params("parallel", "arbitrary"),
        name="swa_attention",
    )(slopes, sinks, qkv, qkv, qkv, qkv, qkv, pos_b, pos_row, pos_row)


def _merge_kernel(mla_ref, swa_lo_ref, swa_hi_ref, fox_ref, w0_ref, w1_ref, w2_ref, w3_ref,
                  g0_ref, g1_ref, g2_ref, o_ref, *, chunks):
    for rs in _row_chunks(o_ref.shape[0], chunks):
        acc = g0_ref[rs, :].astype(F32) * _dot(mla_ref[rs, :], w0_ref[...])
        swa = _dot(swa_lo_ref[rs, :], w1_ref[...]) + _dot(swa_hi_ref[rs, :], w2_ref[...])
        acc = acc + g1_ref[rs, :].astype(F32) * swa
        acc = acc + g2_ref[rs, :].astype(F32) * _dot(fox_ref[rs, :], w3_ref[...])
        o_ref[rs, :] = acc.astype(o_ref.dtype)


def gated_merge(o_mla, o_swa, o_fox, w_branch, layer, gates, tm=1024, tn=512, chunks=2):
    m, kb = o_mla.shape
    d = w_branch.shape[2]
    nj = d // tn
    assert o_swa.shape[1] == 2 * kb and o_fox.shape[1] == kb and w_branch.shape[1] == 4 * kb
    a_spec = lambda cb: pl.BlockSpec((tm, kb), lambda i, j: (i, cb))
    w_spec = lambda rb: pl.BlockSpec((None, kb, tn), lambda i, j: (layer, rb, j))
    g_spec = lambda br: pl.BlockSpec((tm, tn), lambda i, j: (i, br * nj + j))
    return pl.pallas_call(
        functools.partial(_merge_kernel, chunks=chunks),
        grid=(m // tm, nj),
        in_specs=[a_spec(0), a_spec(0), a_spec(1), a_spec(0)] + [w_spec(rb) for rb in range(4)]
                 + [g_spec(br) for br in range(3)],
        out_specs=pl.BlockSpec((tm, tn), lambda i, j: (i, j)),
        out_shape=jax.ShapeDtypeStruct((m, d), BF16),
        compiler_params=_params("parallel", "arbitrary"),
        name="gated_merge",
    )(o_mla, o_swa, o_swa, o_fox, w_branch, w_branch, w_branch, w_branch, gates, gates, gates)


def _ffn_up_kernel(h_ref, r_ref, wg_ref, wu_ref, o_ref, *, chunks):
    for rs in _row_chunks(h_ref.shape[0], chunks):
        r = r_ref[rs, 0:1]
        gate = _dot(h_ref[rs, :], wg_ref[...]) * r
        up = _dot(h_ref[rs, :], wu_ref[...]) * r
        o_ref[rs, :] = (gate * _sigmoid(gate) * up).astype(o_ref.dtype)


def ffn_up(h, r, w_gate_up, layer, tm=2048, tn=256, chunks=4):
    m, d = h.shape
    n = w_gate_up.shape[2] // 2
    assert n % tn == 0
    nj = n // tn
    return pl.pallas_call(
        functools.partial(_ffn_up_kernel, chunks=chunks),
        grid=(m // tm, nj),
        in_specs=[pl.BlockSpec((tm, d), lambda i, j: (i, 0)),
                  pl.BlockSpec((tm, LANES), lambda i, j: (i, 0)),
                  pl.BlockSpec((None, d, tn), lambda i, j: (layer, 0, j)),
                  pl.BlockSpec((None, d, tn), lambda i, j: (layer, 0, nj + j))],
        out_specs=pl.BlockSpec((tm, tn), lambda i, j: (i, j)),
        out_shape=jax.ShapeDtypeStruct((m, n), BF16),
        compiler_params=_params("parallel", "arbitrary"),
        name="ffn_up",
    )(h, r, w_gate_up, w_gate_up)


def _regroup_kernel(main_ref, next_ref, o_ref, *, shift):
    x = jnp.concatenate([main_ref[...], next_ref[...]], axis=1)
    o_ref[...] = x[:, shift:shift + o_ref.shape[1]].astype(o_ref.dtype)


def regroup_cast(w, col0, width, tile, tk=1024):
    depth, k, _ = w.shape
    shift = col0 % LANES
    base = col0 - shift
    assert shift and base % tile == 0 and width % tile == 0 and tile % LANES == 0 and k % tk == 0
    return pl.pallas_call(
        functools.partial(_regroup_kernel, shift=shift),
        grid=(depth, k // tk, width // tile),
        in_specs=[pl.BlockSpec((None, tk, tile), lambda l, r, c: (l, r, base // tile + c)),
                  pl.BlockSpec((None, tk, LANES), lambda l, r, c: (l, r, (base + (c + 1) * tile) // LANES))],
        out_specs=pl.BlockSpec((None, tk, tile), lambda l, r, c: (l, r, c)),
        out_shape=jax.ShapeDtypeStruct((depth, k, width), BF16),
        compiler_params=_params("parallel", "parallel", "arbitrary"),
        name="regroup_cast",
    )(w, w)


def _input_proj_weights(w_in):
    depth, d, _ = w_in.shape
    o = 0
    cuts = {}
    for name, width in (("cq", MLA_Q_LORA), ("ckv", MLA_KV_LORA), ("kr", MLA_ROPE), ("swa", SWA_WIDTH),
                        ("fox", FOX_WIDTH), ("z", FOX_HEADS), ("gate", 3 * d)):
        cuts[name] = (o, o + width)
        o += width
    sl = lambda name: w_in[:, :, cuts[name][0]:cuts[name][1]]
    half = MLA_ROPE // 2
    kr = sl("kr")
    kr_swapped = jnp.concatenate([kr[:, :, half:], kr[:, :, :half]], axis=2)
    z_pad = jnp.zeros((depth, d, LANES - FOX_HEADS), w_in.dtype)
    w_lat = jnp.concatenate([sl("cq"), sl("ckv"), kr, kr_swapped, sl("z"), z_pad], axis=2).astype(BF16)
    w_gate = regroup_cast(w_in, cuts["gate"][0], 3 * d, 1024)
    w_fox = regroup_cast(w_in, cuts["fox"][0], FOX_WIDTH, 1024)
    w_swa = regroup_cast(w_in, cuts["swa"][0], SWA_WIDTH, 512)
    return w_gate, w_fox, w_swa, w_lat


def _q_up_weights(w_uq):
    depth = w_uq.shape[0]
    half = MLA_ROPE // 2
    uq = w_uq.reshape(depth, MLA_Q_LORA, MLA_HEADS, MLA_QK)
    rope = uq[..., MLA_NOPE:]
    rope_swapped = jnp.concatenate([rope[..., half:], rope[..., :half]], axis=-1)
    w_q = jnp.concatenate([uq[..., :MLA_NOPE], rope, rope_swapped], axis=-1)
    return w_q.reshape(depth, MLA_Q_LORA, MLA_HEADS * 2 * LANES).astype(BF16)


def _rope_consts():
    half = MLA_ROPE // 2
    inv_freq = ROPE_THETA ** (-jnp.arange(half, dtype=F32) / half)
    zeros = jnp.zeros((LANES - MLA_ROPE,), F32)
    ones = jnp.ones((half,), F32)
    rows = jnp.stack([jnp.concatenate([inv_freq, inv_freq, zeros]),
                      jnp.concatenate([ones, ones, zeros]),
                      jnp.concatenate([-ones, ones, zeros])])
    return jnp.concatenate([rows, jnp.zeros((5, LANES), F32)], axis=0)


def kernel(x, positions, g_mix_pre, g_mix_post, g_ffn_pre, g_ffn_post, w_in, g_q_lora, w_uq, g_kv_lora, w_ukv,
           b_forget, swa_sinks, w_branch, w_out, w_gate_up, w_down):
    batch, seq, d = x.shape
    depth = w_in.shape[0]
    m = batch * seq
    xs = x.reshape(m, d)
    pos_b = jnp.broadcast_to(positions.reshape(m, 1), (m, LANES))
    pos_row = positions.reshape(m // WINDOW, 1, WINDOW)
    slopes = 2.0 ** (-8.0 * jnp.arange(1, SWA_HEADS + 1, dtype=F32) / SWA_HEADS)
    cos_t, sin_t = rope_tables(pos_b, _rope_consts())
    vec = lambda g: g.reshape(1, -1)

    w_gate, w_fox, w_swa, w_lat = _input_proj_weights(w_in)
    w_q = _q_up_weights(w_uq)
    w_kv = w_ukv.astype(BF16)
    w_br = w_branch.astype(BF16)
    w_o = w_out.astype(BF16)
    w_gu = w_gate_up.astype(BF16)
    w_dn = w_down.astype(BF16)
    b_rows = jnp.zeros((depth, 1, LANES), F32).at[:, 0, :FOX_HEADS].set(b_forget)

    h, r = prenorm(xs, vec(g_mix_pre[0]))
    for l in range(depth):
        gates = matmul(h, w_gate, l, 0, 3 * d, BF16, 1024, 1024, row_scale=r, act="sigmoid", chunks=4,
                       name="gate_proj")
        qkv_fox = matmul(h, w_fox, l, 0, FOX_WIDTH, BF16, 1024, 1024, row_scale=r, chunks=2, name="fox_proj")
        qkv_swa = matmul(h, w_swa, l, 0, SWA_WIDTH, BF16, 1024, SWA_WIDTH // 2, row_scale=r, chunks=2,
                         name="swa_proj")
        cqn, ckvn, k_rot, z_slab = latent_proj(h, r, w_lat, l, 0, vec(g_q_lora[l]), vec(g_kv_lora[l]),
                                               cos_t, sin_t)
        q_mla = q_up_proj(cqn, w_q, l, cos_t, sin_t)
        kv_mla = matmul(ckvn, w_kv, l, 0, w_kv.shape[2], BF16, 1024, 1024, name="kv_up_proj")
        f_t = fox_decay(z_slab, b_rows[l], batch, seq)
        o_mla = mla_attention(q_mla, kv_mla, k_rot, batch, seq)
        o_swa = swa_attention(qkv_swa, pos_b, pos_row, slopes, swa_sinks[l].astype(F32), batch, seq)
        o_fox = fox_attention(qkv_fox, f_t, batch, seq)
        merged = gated_merge(o_mla, o_swa, o_fox, w_br, l, gates)
        xs, h2, r2 = proj_norm(merged, w_o, l, xs, vec(g_mix_post[l]), vec(g_ffn_pre[l]), 512, 1024)
        act = ffn_up(h2, r2, w_gu, l)
        g_next = vec(g_mix_pre[l + 1]) if l + 1 < depth else None
        xs, h, r = proj_norm(act, w_dn, l, xs, vec(g_ffn_post[l]), g_next, 512, 256)
    return xs.reshape(batch, seq, d)
```

```python
import functools
import math
from typing import Any, NamedTuple

import jax
import jax.numpy as jnp
from jax import lax
from jax.experimental import pallas as pl
from jax.experimental.pallas import tpu as pltpu

F32 = jnp.float32
BF16 = jnp.bfloat16

VMEM_LIMIT_BYTES = 56 * 1024 * 1024
LANES = 128

HEAD_DIM = 128
MLA_HEADS = 8
MLA_Q_LORA = 1024
MLA_KV_LORA = 512
MLA_NOPE = 128
MLA_ROPE = 64
MLA_QK = MLA_NOPE + MLA_ROPE
SWA_HEADS = 16
SWA_KV_HEADS = 2
SWA_GROUP = SWA_HEADS // SWA_KV_HEADS
WINDOW = 128
FOX_HEADS = 8
ROPE_THETA = 10000.0
RMS_EPS = 1e-6
LOG2_E = math.log2(math.e)
Q_TILE = 256
HEADS_PER_STEP = 2

SWA_WIDTH = (SWA_HEADS + 2 * SWA_KV_HEADS) * HEAD_DIM
FOX_WIDTH = 3 * FOX_HEADS * HEAD_DIM
LAT_WIDTH = MLA_Q_LORA + MLA_KV_LORA + 2 * LANES


def _params(*sem):
    return pltpu.CompilerParams(dimension_semantics=sem, vmem_limit_bytes=VMEM_LIMIT_BYTES)


def _inv_rms(x):
    return lax.rsqrt(jnp.mean(x * x, axis=-1, keepdims=True) + RMS_EPS)


def _dot(a, b):
    return jnp.dot(a, b, preferred_element_type=F32)


def _dot_nt(a, b):
    return lax.dot_general(a, b, (((1,), (1,)), ((), ())), preferred_element_type=F32)


def _sigmoid(x):
    return 1.0 / (1.0 + jnp.exp(-x))


def _row_chunks(rows, chunks):
    step = rows // chunks
    return [slice(c * step, (c + 1) * step) for c in range(chunks)]


def _rope_table_kernel(pos_ref, c_ref, cos_ref, sin_ref):
    ang = pos_ref[...].astype(F32) * c_ref[0:1, :]
    cos_ref[...] = jnp.cos(ang) * c_ref[1:2, :]
    sin_ref[...] = jnp.sin(ang) * c_ref[2:3, :]


def rope_tables(pos_b, consts, tm=2048):
    m = pos_b.shape[0]
    spec = pl.BlockSpec((tm, LANES), lambda i: (i, 0))
    return pl.pallas_call(
        _rope_table_kernel,
        grid=(m // tm,),
        in_specs=[spec, pl.BlockSpec((8, LANES), lambda i: (0, 0))],
        out_specs=[spec, spec],
        out_shape=[jax.ShapeDtypeStruct((m, LANES), F32)] * 2,
        compiler_params=_params("parallel"),
        name="rope_tables",
    )(pos_b, consts)


def _prenorm_kernel(x_ref, g_ref, h_ref, r_ref):
    x = x_ref[...]
    h_ref[...] = (x * g_ref[...]).astype(h_ref.dtype)
    r_ref[...] = jnp.broadcast_to(_inv_rms(x), r_ref.shape)


def prenorm(x, gain, tm=256):
    m, d = x.shape
    return pl.pallas_call(
        _prenorm_kernel,
        grid=(m // tm,),
        in_specs=[pl.BlockSpec((tm, d), lambda i: (i, 0)), pl.BlockSpec((1, d), lambda i: (0, 0))],
        out_specs=[pl.BlockSpec((tm, d), lambda i: (i, 0)), pl.BlockSpec((tm, LANES), lambda i: (i, 0))],
        out_shape=[jax.ShapeDtypeStruct((m, d), BF16), jax.ShapeDtypeStruct((m, LANES), F32)],
        compiler_params=_params("parallel"),
        name="prenorm",
    )(x, gain)


def _matmul_kernel(*refs, act, row_scaled, col_scaled, chunks):
    refs = list(refs)
    a_ref = refs.pop(0)
    r_ref = refs.pop(0) if row_scaled else None
    c_ref = refs.pop(0) if col_scaled else None
    w_ref, o_ref = refs
    for rs in _row_chunks(a_ref.shape[0], chunks):
        acc = _dot(a_ref[rs, :], w_ref[...])
        if row_scaled:
            acc = acc * r_ref[rs, 0:1]
        if col_scaled:
            acc = acc * c_ref[...]
        if act == "sigmoid":
            acc = _sigmoid(acc)
        o_ref[rs, :] = acc.astype(o_ref.dtype)


def matmul(a, w, layer, n, out_dtype, tm, tn, row_scale=None, col_scale=None, act=None, chunks=1, name="matmul"):
    m, k = a.shape
    assert m % tm == 0 and n % tn == 0 and w.shape[1:] == (k, n)
    in_specs = [pl.BlockSpec((tm, k), lambda i, j: (i, 0))]
    args = [a]
    if row_scale is not None:
        in_specs.append(pl.BlockSpec((tm, LANES), lambda i, j: (i, 0)))
        args.append(row_scale)
    if col_scale is not None:
        in_specs.append(pl.BlockSpec((1, tn), lambda i, j: (0, j)))
        args.append(col_scale)
    in_specs.append(pl.BlockSpec((None, k, tn), lambda i, j: (layer, 0, j)))
    args.append(w)
    return pl.pallas_call(
        functools.partial(_matmul_kernel, act=act, row_scaled=row_scale is not None,
                          col_scaled=col_scale is not None, chunks=chunks),
        grid=(m // tm, n // tn),
        in_specs=in_specs,
        out_specs=pl.BlockSpec((tm, tn), lambda i, j: (i, j)),
        out_shape=jax.ShapeDtypeStruct((m, n), out_dtype),
        compiler_params=_params("parallel", "arbitrary"),
        name=name,
    )(*args)


def _proj_norm_kernel(*refs, n_i, n_j, d, emit_next):
    if emit_next:
        a_ref, w_ref, x_ref, gp_ref, gn_ref, xo_ref, h_ref, r_ref, y_scr, acc_scr, done_scr, acc2_scr = refs
    else:
        a_ref, w_ref, x_ref, gp_ref, xo_ref, y_scr, acc_scr, done_scr = refs
    i = pl.program_id(0)
    j = pl.program_id(1)

    @pl.when((i == 0) & (j == 0))
    def _():
        y_scr[...] = jnp.zeros_like(y_scr)
        acc_scr[...] = jnp.zeros_like(acc_scr)
        done_scr[...] = jnp.zeros_like(done_scr)
        if emit_next:
            acc2_scr[...] = jnp.zeros_like(acc2_scr)

    first = j == 0
    y_prev = y_scr[j]
    y_new = _dot(a_ref[...], w_ref[...])
    y_scr[j] = y_new
    part = jnp.sum(y_new * y_new, axis=-1, keepdims=True)
    acc_old = acc_scr[...]
    done = jnp.where(first, acc_old, done_scr[...])
    done_scr[...] = done
    acc_scr[...] = jnp.where(first, part, acc_old + part)

    x_new = x_ref[...] + y_prev * lax.rsqrt(done * (1.0 / d) + RMS_EPS) * gp_ref[...]
    xo_ref[...] = x_new
    if emit_next:
        h_ref[...] = (x_new * gn_ref[...]).astype(h_ref.dtype)
        part2 = jnp.sum(x_new * x_new, axis=-1, keepdims=True)
        acc2 = jnp.where(first, part2, acc2_scr[...] + part2)
        acc2_scr[...] = acc2
        r_ref[...] = jnp.broadcast_to(lax.rsqrt(acc2 * (1.0 / d) + RMS_EPS), r_ref.shape)


def proj_norm(a, w, layer, x, g_post, g_next, tm, tn):
    m, k = a.shape
    d = w.shape[2]
    n_i, n_j = m // tm, d // tn
    emit_next = g_next is not None
    prev = lambda i: jnp.maximum(i - 1, 0)
    col = pl.BlockSpec((tm, tn), lambda i, j: (prev(i), j))
    col_out = pl.BlockSpec((tm, tn), lambda i, j: (prev(i), jnp.where(i == 0, 0, j)))
    vec = pl.BlockSpec((1, tn), lambda i, j: (0, j))
    in_specs = [pl.BlockSpec((tm, k), lambda i, j: (jnp.minimum(i, n_i - 1), 0)),
                pl.BlockSpec((None, k, tn), lambda i, j: (layer, 0, j)),
                col, vec]
    args = [a, w, x, g_post]
    out_specs = [col_out]
    out_shape = [jax.ShapeDtypeStruct((m, d), F32)]
    scratch = [pltpu.VMEM((n_j, tm, tn), F32), pltpu.VMEM((tm, 1), F32), pltpu.VMEM((tm, 1), F32)]
    if emit_next:
        in_specs.append(vec)
        args.append(g_next)
        out_specs += [col_out, pl.BlockSpec((tm, LANES), lambda i, j: (prev(i), 0))]
        out_shape += [jax.ShapeDtypeStruct((m, d), BF16), jax.ShapeDtypeStruct((m, LANES), F32)]
        scratch.append(pltpu.VMEM((tm, 1), F32))
    res = pl.pallas_call(
        functools.partial(_proj_norm_kernel, n_i=n_i, n_j=n_j, d=d, emit_next=emit_next),
        grid=(n_i + 1, n_j),
        in_specs=in_specs,
        out_specs=out_specs,
        out_shape=out_shape,
        scratch_shapes=scratch,
        compiler_params=_params("arbitrary", "arbitrary"),
        name="proj_norm" if emit_next else "proj_norm_last",
    )(*args)
    return res if emit_next else (res[0], None, None)


def _latent_kernel(h_ref, r_ref, w_ref, gq_ref, gkv_ref, cos_ref, sin_ref, cq_ref, ckv_ref, kr_ref, z_ref,
                   *, chunks):
    q_end = MLA_Q_LORA
    kv_end = q_end + MLA_KV_LORA
    for rs in _row_chunks(h_ref.shape[0], chunks):
        acc = _dot(h_ref[rs, :], w_ref[...]) * r_ref[rs, 0:1]
        cq = acc[:, :q_end]
        ckv = acc[:, q_end:kv_end]
        cq_ref[rs, :] = (cq * _inv_rms(cq) * gq_ref[...]).astype(cq_ref.dtype)
        ckv_ref[rs, :] = (ckv * _inv_rms(ckv) * gkv_ref[...]).astype(ckv_ref.dtype)
        slab = acc[:, kv_end:kv_end + LANES]
        rot = slab * cos_ref[rs, :] + pltpu.roll(slab, LANES // 2, 1) * sin_ref[rs, :]
        kr_ref[rs, :] = rot.astype(kr_ref.dtype)
        z_ref[rs, :] = acc[:, kv_end + LANES:kv_end + 2 * LANES]


def latent_proj(h, r, w_all, layer, col_block, g_q, g_kv, cos_t, sin_t, tm=512, chunks=2):
    m, d = h.shape
    n = LAT_WIDTH
    row = lambda width: pl.BlockSpec((tm, width), lambda i: (i, 0))
    const = lambda c: pl.BlockSpec((1, c), lambda i: (0, 0))
    return pl.pallas_call(
        functools.partial(_latent_kernel, chunks=chunks),
        grid=(m // tm,),
        in_specs=[row(d), row(LANES), pl.BlockSpec((None, d, n), lambda i: (layer, 0, col_block)),
                  const(MLA_Q_LORA), const(MLA_KV_LORA), row(LANES), row(LANES)],
        out_specs=[row(MLA_Q_LORA), row(MLA_KV_LORA), row(LANES), row(LANES)],
        out_shape=[jax.ShapeDtypeStruct((m, MLA_Q_LORA), BF16), jax.ShapeDtypeStruct((m, MLA_KV_LORA), BF16),
                   jax.ShapeDtypeStruct((m, LANES), BF16), jax.ShapeDtypeStruct((m, LANES), F32)],
        compiler_params=_params("parallel"),
        name="latent_proj",
    )(h, r, w_all, g_q, g_kv, cos_t, sin_t)


def _q_up_kernel(a_ref, w_ref, cos_ref, sin_ref, o_ref, *, scale):
    acc = _dot(a_ref[...], w_ref[...])
    c = cos_ref[...]
    s = sin_ref[...]
    for h in range(MLA_HEADS):
        base = 2 * LANES * h
        nope = acc[:, base:base + LANES]
        slab = acc[:, base + LANES:base + 2 * LANES]
        rot = slab * c + pltpu.roll(slab, LANES // 2, 1) * s
        o_ref[:, base:base + LANES] = (nope * scale).astype(o_ref.dtype)
        o_ref[:, base + LANES:base + 2 * LANES] = (rot * scale).astype(o_ref.dtype)


def q_up_proj(cqn, w_uq, layer, cos_t, sin_t, tm=1024):
    m, k = cqn.shape
    n = w_uq.shape[2]
    return pl.pallas_call(
        functools.partial(_q_up_kernel, scale=MLA_QK ** -0.5 * LOG2_E),
        grid=(m // tm,),
        in_specs=[pl.BlockSpec((tm, k), lambda i: (i, 0)), pl.BlockSpec((None, k, n), lambda i: (layer, 0, 0)),
                  pl.BlockSpec((tm, LANES), lambda i: (i, 0)), pl.BlockSpec((tm, LANES), lambda i: (i, 0))],
        out_specs=pl.BlockSpec((tm, n), lambda i: (i, 0)),
        out_shape=jax.ShapeDtypeStruct((m, n), BF16),
        compiler_params=_params("parallel"),
        name="q_up_proj",
    )(cqn, w_uq, cos_t, sin_t)


class _Head(NamedTuple):
    q: Any
    q_cols: slice
    k: Any
    k_cols: slice
    v: Any
    v_cols: slice
    o: Any
    o_cols: slice
    fq: Any = None
    fk_row: Any = None


def _causal_blocks(heads, seq, tq):
    row = lax.broadcasted_iota(jnp.int32, (tq, tq), 0)
    col = lax.broadcasted_iota(jnp.int32, (tq, tq), 1)
    tri = col <= row
    for i in range(seq // tq):
        lo, hi = i * tq, (i + 1) * tq
        for hd in heads:
            qi = hd.q[lo:hi, hd.q_cols]
            s_d = _dot_nt(qi, hd.k[lo:hi, hd.k_cols])
            if hd.fk_row is not None:
                s_d = s_d - hd.fk_row[:, lo:hi]
            s_d = jnp.where(tri, s_d, -jnp.inf)
            mx = jnp.max(s_d, axis=-1, keepdims=True)
            if i > 0:
                s_o = _dot_nt(qi, hd.k[:lo, hd.k_cols])
                if hd.fk_row is not None:
                    s_o = s_o - hd.fk_row[:, :lo]
                mx = jnp.maximum(mx, jnp.max(s_o, axis=-1, keepdims=True))
            if hd.fq is not None:
                fq = hd.fq[lo:hi, 0:1]
                off = fq - (mx + fq)
            else:
                off = -mx
            p_d = jnp.exp2(s_d + off)
            den = jnp.sum(p_d, axis=-1, keepdims=True)
            acc = _dot(p_d.astype(BF16), hd.v[lo:hi, hd.v_cols])
            if i > 0:
                p_o = jnp.exp2(s_o + off)
                den = den + jnp.sum(p_o, axis=-1, keepdims=True)
                acc = acc + _dot(p_o.astype(BF16), hd.v[:lo, hd.v_cols])
            hd.o[lo:hi, hd.o_cols] = (acc / den).astype(hd.o.dtype)


def _mla_attn_kernel(q_ref, kv_ref, kr_ref, o_ref, k_scr, *, seq, tq, hps):
    heads = []
    for u in range(hps):
        base = 2 * LANES * u
        k_scr[u, :, :LANES] = kv_ref[:, base:base + LANES]
        k_scr[u, :, LANES:] = kr_ref[...]
        heads.append(_Head(q_ref, slice(base, base + 2 * LANES), k_scr.at[u], slice(None),
                           kv_ref, slice(base + LANES, base + 2 * LANES), o_ref, slice(LANES * u, LANES * (u + 1))))
    _causal_blocks(heads, seq, tq)


def mla_attention(q, kv, k_rot, batch, seq, hps=HEADS_PER_STEP):
    m = q.shape[0]
    return pl.pallas_call(
        functools.partial(_mla_attn_kernel, seq=seq, tq=Q_TILE, hps=hps),
        grid=(batch, MLA_HEADS // hps),
        in_specs=[pl.BlockSpec((seq, 2 * LANES * hps), lambda b, g: (b, g)),
                  pl.BlockSpec((seq, 2 * LANES * hps), lambda b, g: (b, g)),
                  pl.BlockSpec((seq, LANES), lambda b, g: (b, 0))],
        out_specs=pl.BlockSpec((seq, LANES * hps), lambda b, g: (b, g)),
        out_shape=jax.ShapeDtypeStruct((m, MLA_HEADS * LANES), BF16),
        scratch_shapes=[pltpu.VMEM((hps, seq, 2 * LANES), BF16)],
        compiler_params=_params("parallel", "arbitrary"),
        name="mla_attention",
    )(q, kv, k_rot)


def _fox_attn_kernel(q_ref, k_ref, v_ref, ft_ref, o_ref, fq_scr, *, seq, tq, hps):
    g = pl.program_id(1)
    heads = []
    for u in range(hps):
        fk_row = ft_ref[0, pl.ds(g * hps + u, 1), :]
        fq_scr[u] = jnp.transpose(jnp.broadcast_to(fk_row, (LANES, seq)))
        cols = slice(LANES * u, LANES * (u + 1))
        heads.append(_Head(q_ref, cols, k_ref, cols, v_ref, cols, o_ref, cols, fq_scr.at[u], fk_row))
    _causal_blocks(heads, seq, tq)


def fox_attention(qkv, f_t, batch, seq, hps=HEADS_PER_STEP):
    m = qkv.shape[0]
    nh = FOX_HEADS
    ng = nh // hps
    blk = lambda part: pl.BlockSpec((seq, LANES * hps), lambda b, g: (b, part * ng + g))
    return pl.pallas_call(
        functools.partial(_fox_attn_kernel, seq=seq, tq=Q_TILE, hps=hps),
        grid=(batch, ng),
        in_specs=[blk(0), blk(1), blk(2), pl.BlockSpec((1, nh, seq), lambda b, g: (b, 0, 0))],
        out_specs=pl.BlockSpec((seq, LANES * hps), lambda b, g: (b, g)),
        out_shape=jax.ShapeDtypeStruct((m, nh * LANES), BF16),
        scratch_shapes=[pltpu.VMEM((hps, seq, LANES), F32)],
        compiler_params=_params("parallel", "arbitrary"),
        name="fox_attention",
    )(qkv, qkv, qkv, f_t)


def _fox_decay_kernel(z_ref, b_ref, ft_ref, *, seq):
    x = z_ref[...] + b_ref[...]
    log_f = jnp.minimum(x, 0.0) - jnp.log1p(jnp.exp(-jnp.abs(x)))
    t = jnp.transpose(log_f)[0:FOX_HEADS, :]
    lane = lax.broadcasted_iota(jnp.int32, t.shape, 1)
    shift = 1
    while shift < seq:
        t = t + jnp.where(lane >= shift, pltpu.roll(t, shift, 1), 0.0)
        shift *= 2
    ft_ref[0] = t * LOG2_E


def fox_decay(z_slab, b_row, batch, seq):
    return pl.pallas_call(
        functools.partial(_fox_decay_kernel, seq=seq),
        grid=(batch,),
        in_specs=[pl.BlockSpec((seq, LANES), lambda b: (b, 0)), pl.BlockSpec((1, LANES), lambda b: (0, 0))],
        out_specs=pl.BlockSpec((1, FOX_HEADS, seq), lambda b: (b, 0, 0)),
        out_shape=jax.ShapeDtypeStruct((batch, FOX_HEADS, seq), F32),
        compiler_params=_params("parallel"),
        name="fox_decay",
    )(z_slab, b_row)


def _swa_kernel(slope_ref, sink_ref, q_ref, kp_ref, kc_ref, vp_ref, vc_ref, qpos_ref, pp_ref, pc_ref, o_ref):
    n = pl.program_id(1)
    w = WINDOW
    i = lax.broadcasted_iota(jnp.int32, (w, w), 0)
    j = lax.broadcasted_iota(jnp.int32, (w, w), 1)
    valid = jnp.concatenate([(j > i) & (n > 0), j <= i], axis=1)
    qpos = qpos_ref[:, 0:1]
    dist = jnp.concatenate([qpos - pp_ref[0], qpos - pc_ref[0]], axis=1).astype(F32)
    neg_dist = jnp.where(valid, -dist, -jnp.inf)
    for kh in range(SWA_KV_HEADS):
        ks = slice(kh * HEAD_DIM, (kh + 1) * HEAD_DIM)
        k_win = jnp.concatenate([kp_ref[:, ks], kc_ref[:, ks]], axis=0)
        v_win = jnp.concatenate([vp_ref[:, ks], vc_ref[:, ks]], axis=0)
        for g in range(SWA_GROUP):
            hd = kh * SWA_GROUP + g
            hs = slice(hd * HEAD_DIM, (hd + 1) * HEAD_DIM)
            s = _dot_nt(q_ref[:, hs], k_win) + slope_ref[hd] * neg_dist
            sink = sink_ref[hd]
            mx = jnp.maximum(jnp.max(s, axis=-1, keepdims=True), sink)
            e = jnp.exp2(s - mx)
            den = jnp.sum(e, axis=-1, keepdims=True) + jnp.exp2(sink - mx)
            o_ref[:, hs] = (_dot(e.astype(BF16), v_win) / den).astype(o_ref.dtype)


def swa_attention(qkv, pos_b, pos_row, slopes, sinks, batch, seq):
    m = qkv.shape[0]
    w = WINDOW
    nb = seq // w
    qw = SWA_HEADS * HEAD_DIM
    kvw = SWA_KV_HEADS * HEAD_DIM
    cur = lambda b, n: b * nb + n
    prev = lambda b, n: b * nb + jnp.maximum(n - 1, 0)
    smem = pl.BlockSpec(memory_space=pltpu.SMEM)
    return pl.pallas_call(
        _swa_kernel,
        grid=(batch, nb),
        in_specs=[smem, smem,
                  pl.BlockSpec((w, qw), lambda b, n: (cur(b, n), 0)),
                  pl.BlockSpec((w, kvw), lambda b, n: (prev(b, n), qw // kvw)),
                  pl.BlockSpec((w, kvw), lambda b, n: (cur(b, n), qw // kvw)),
                  pl.BlockSpec((w, kvw), lambda b, n: (prev(b, n), qw // kvw + 1)),
                  pl.BlockSpec((w, kvw), lambda b, n: (cur(b, n), qw // kvw + 1)),
                  pl.BlockSpec((w, LANES), lambda b, n: (cur(b, n), 0)),
                  pl.BlockSpec((1, 1, w), lambda b, n: (prev(b, n), 0, 0)),
                  pl.BlockSpec((1, 1, w), lambda b, n: (cur(b, n), 0, 0))],
        out_specs=pl.BlockSpec((w, qw), lambda b, n: (cur(b, n), 0)),
        out_shape=jax.ShapeDtypeStruct((m, qw), BF16),
        compiler_params=_params("parallel", "arbitrary"),
        name="swa_attention",
    )(slopes, sinks, qkv, qkv, qkv, qkv, qkv, pos_b, pos_row, pos_row)


def _merge_kernel(mla_ref, swa_lo_ref, swa_hi_ref, fox_ref, w0_ref, w1_ref, w2_ref, w3_ref,
                  g0_ref, g1_ref, g2_ref, o_ref, *, chunks):
    for rs in _row_chunks(o_ref.shape[0], chunks):
        acc = g0_ref[rs, :].astype(F32) * _dot(mla_ref[rs, :], w0_ref[...])
        swa = _dot(swa_lo_ref[rs, :], w1_ref[...]) + _dot(swa_hi_ref[rs, :], w2_ref[...])
        acc = acc + g1_ref[rs, :].astype(F32) * swa
        acc = acc + g2_ref[rs, :].astype(F32) * _dot(fox_ref[rs, :], w3_ref[...])
        o_ref[rs, :] = acc.astype(o_ref.dtype)


def gated_merge(o_mla, o_swa, o_fox, w_branch, layer, gates, tm=1024, tn=512, chunks=2):
    m, kb = o_mla.shape
    d = w_branch.shape[2]
    nj = d // tn
    assert o_swa.shape[1] == 2 * kb and o_fox.shape[1] == kb and w_branch.shape[1] == 4 * kb
    a_spec = lambda cb: pl.BlockSpec((tm, kb), lambda i, j: (i, cb))
    w_spec = lambda rb: pl.BlockSpec((None, kb, tn), lambda i, j: (layer, rb, j))
    g_spec = lambda br: pl.BlockSpec((tm, tn), lambda i, j: (i, br * nj + j))
    return pl.pallas_call(
        functools.partial(_merge_kernel, chunks=chunks),
        grid=(m // tm, nj),
        in_specs=[a_spec(0), a_spec(0), a_spec(1), a_spec(0)] + [w_spec(rb) for rb in range(4)]
                 + [g_spec(br) for br in range(3)],
        out_specs=pl.BlockSpec((tm, tn), lambda i, j: (i, j)),
        out_shape=jax.ShapeDtypeStruct((m, d), BF16),
        compiler_params=_params("parallel", "arbitrary"),
        name="gated_merge",
    )(o_mla, o_swa, o_swa, o_fox, w_branch, w_branch, w_branch, w_branch, gates, gates, gates)


def _ffn_up_kernel(h_ref, r_ref, wg_ref, wu_ref, o_ref, *, chunks):
    for rs in _row_chunks(h_ref.shape[0], chunks):
        r = r_ref[rs, 0:1]
        gate = _dot(h_ref[rs, :], wg_ref[...]) * r
        up = _dot(h_ref[rs, :], wu_ref[...]) * r
        o_ref[rs, :] = (gate * _sigmoid(gate) * up).astype(o_ref.dtype)


def ffn_up(h, r, w_gate_up, layer, tm=2048, tn=256, chunks=4):
    m, d = h.shape
    n = w_gate_up.shape[2] // 2
    assert n % tn == 0
    nj = n // tn
    return pl.pallas_call(
        functools.partial(_ffn_up_kernel, chunks=chunks),
        grid=(m // tm, nj),
        in_specs=[pl.BlockSpec((tm, d), lambda i, j: (i, 0)),
                  pl.BlockSpec((tm, LANES), lambda i, j: (i, 0)),
                  pl.BlockSpec((None, d, tn), lambda i, j: (layer, 0, j)),
                  pl.BlockSpec((None, d, tn), lambda i, j: (layer, 0, nj + j))],
        out_specs=pl.BlockSpec((tm, tn), lambda i, j: (i, j)),
        out_shape=jax.ShapeDtypeStruct((m, n), BF16),
        compiler_params=_params("parallel", "arbitrary"),
        name="ffn_up",
    )(h, r, w_gate_up, w_gate_up)


def _regroup_kernel(main_ref, next_ref, o_ref, *, shift):
    x = jnp.concatenate([main_ref[shift:, :], next_ref[:shift, :]], axis=0)
    o_ref[...] = jnp.transpose(x).astype(o_ref.dtype)


def regroup_cast(w_t, row0, width, tile=512, tk=2048):
    depth, _, k = w_t.shape
    shift = row0 % LANES
    base = row0 - shift
    assert shift % 8 == 0 and shift and base % tile == 0 and width % tile == 0 and k % tk == 0
    return pl.pallas_call(
        functools.partial(_regroup_kernel, shift=shift),
        grid=(depth, k // tk, width // tile),
        in_specs=[pl.BlockSpec((None, tile, tk), lambda l, kb, c: (l, base // tile + c, kb)),
                  pl.BlockSpec((None, LANES, tk), lambda l, kb, c: (l, (base + (c + 1) * tile) // LANES, kb))],
        out_specs=pl.BlockSpec((None, tk, tile), lambda l, kb, c: (l, kb, c)),
        out_shape=jax.ShapeDtypeStruct((depth, k, width), BF16),
        compiler_params=_params("parallel", "parallel", "arbitrary"),
        name="regroup_cast",
    )(w_t, w_t)


def _latent_weight_kernel(main_ref, kr_ref, z_ref, o_ref):
    half = MLA_ROPE // 2
    kr = kr_ref[...]
    pad = jnp.zeros((LANES - FOX_HEADS, kr.shape[1]), kr.dtype)
    tail = jnp.concatenate([kr, kr[half:, :], kr[:half, :], z_ref[...], pad], axis=0)
    n_main = main_ref.shape[0]
    o_ref[:, :n_main] = jnp.transpose(main_ref[...]).astype(o_ref.dtype)
    o_ref[:, n_main:] = jnp.transpose(tail).astype(o_ref.dtype)


def latent_weights(w_t, kr_row, z_row, tk=1024):
    depth, _, k = w_t.shape
    n_main = kr_row
    assert n_main % LANES == 0 and kr_row % MLA_ROPE == 0 and z_row % FOX_HEADS == 0 and FOX_HEADS % 8 == 0
    return pl.pallas_call(
        _latent_weight_kernel,
        grid=(depth, k // tk),
        in_specs=[pl.BlockSpec((None, n_main, tk), lambda l, kb: (l, 0, kb)),
                  pl.BlockSpec((None, MLA_ROPE, tk), lambda l, kb: (l, kr_row // MLA_ROPE, kb)),
                  pl.BlockSpec((None, FOX_HEADS, tk), lambda l, kb: (l, z_row // FOX_HEADS, kb))],
        out_specs=pl.BlockSpec((None, tk, LAT_WIDTH), lambda l, kb: (l, kb, 0)),
        out_shape=jax.ShapeDtypeStruct((depth, k, LAT_WIDTH), BF16),
        compiler_params=_params("parallel", "parallel"),
        name="latent_weights",
    )(w_t, w_t, w_t)


def _input_proj_weights(w_in):
    d = w_in.shape[1]
    w_t = jnp.swapaxes(w_in, 1, 2)
    o = 0
    cuts = {}
    for name, width in (("cq", MLA_Q_LORA), ("ckv", MLA_KV_LORA), ("kr", MLA_ROPE), ("swa", SWA_WIDTH),
                        ("fox", FOX_WIDTH), ("z", FOX_HEADS), ("gate", 3 * d)):
        cuts[name] = (o, o + width)
        o += width
    assert cuts["cq"][0] == 0 and cuts["kr"][0] == MLA_Q_LORA + MLA_KV_LORA
    w_lat = latent_weights(w_t, cuts["kr"][0], cuts["z"][0])
    w_gate = regroup_cast(w_t, cuts["gate"][0], 3 * d)
    w_fox = regroup_cast(w_t, cuts["fox"][0], FOX_WIDTH)
    w_swa = regroup_cast(w_t, cuts["swa"][0], SWA_WIDTH)
    return w_gate, w_fox, w_swa, w_lat


def _q_up_weights(w_uq):
    depth = w_uq.shape[0]
    half = MLA_ROPE // 2
    uq = w_uq.reshape(depth, MLA_Q_LORA, MLA_HEADS, MLA_QK)
    rope = uq[..., MLA_NOPE:]
    rope_swapped = jnp.concatenate([rope[..., half:], rope[..., :half]], axis=-1)
    w_q = jnp.concatenate([uq[..., :MLA_NOPE], rope, rope_swapped], axis=-1)
    return w_q.reshape(depth, MLA_Q_LORA, MLA_HEADS * 2 * LANES).astype(BF16)


def _rope_consts():
    half = MLA_ROPE // 2
    inv_freq = ROPE_THETA ** (-jnp.arange(half, dtype=F32) / half)
    zeros = jnp.zeros((LANES - MLA_ROPE,), F32)
    ones = jnp.ones((half,), F32)
    rows = jnp.stack([jnp.concatenate([inv_freq, inv_freq, zeros]),
                      jnp.concatenate([ones, ones, zeros]),
                      jnp.concatenate([-ones, ones, zeros])])
    return jnp.concatenate([rows, jnp.zeros((5, LANES), F32)], axis=0)


def kernel(x, positions, g_mix_pre, g_mix_post, g_ffn_pre, g_ffn_post, w_in, g_q_lora, w_uq, g_kv_lora, w_ukv,
           b_forget, swa_sinks, w_branch, w_out, w_gate_up, w_down):
    batch, seq, d = x.shape
    depth = w_in.shape[0]
    m = batch * seq
    xs = x.reshape(m, d)
    pos_b = jnp.broadcast_to(positions.reshape(m, 1), (m, LANES))
    pos_row = positions.reshape(m // WINDOW, 1, WINDOW)
    slopes = 2.0 ** (-8.0 * jnp.arange(1, SWA_HEADS + 1, dtype=F32) / SWA_HEADS)
    cos_t, sin_t = rope_tables(pos_b, _rope_consts())
    vec = lambda g: g.reshape(1, -1)

    w_gate, w_fox, w_swa, w_lat = _input_proj_weights(w_in)
    w_q = _q_up_weights(w_uq)
    w_kv = w_ukv.astype(BF16)
    w_br = w_branch.astype(BF16)
    w_o = w_out.astype(BF16)
    w_gu = w_gate_up.astype(BF16)
    w_dn = w_down.astype(BF16)
    b_rows = jnp.zeros((depth, 1, LANES), F32).at[:, 0, :FOX_HEADS].set(b_forget)
    q_scale = HEAD_DIM ** -0.5 * LOG2_E
    qkv_scale = lambda q_width, width: jnp.concatenate(
        [jnp.full((1, q_width), q_scale, F32), jnp.ones((1, width - q_width), F32)], axis=1)
    fox_scale = qkv_scale(FOX_HEADS * HEAD_DIM, FOX_WIDTH)
    swa_scale = qkv_scale(SWA_HEADS * HEAD_DIM, SWA_WIDTH)

    h, r = prenorm(xs, vec(g_mix_pre[0]))
    for l in range(depth):
        gates = matmul(h, w_gate, l, 3 * d, BF16, 1024, 1024, row_scale=r, act="sigmoid", chunks=2,
                       name="gate_proj")
        qkv_fox = matmul(h, w_fox, l, FOX_WIDTH, BF16, 1024, 1024, row_scale=r, col_scale=fox_scale, chunks=2,
                         name="fox_proj")
        qkv_swa = matmul(h, w_swa, l, SWA_WIDTH, BF16, 1024, SWA_WIDTH // 2, row_scale=r, col_scale=swa_scale,
                         chunks=2, name="swa_proj")
        cqn, ckvn, k_rot, z_slab = latent_proj(h, r, w_lat, l, 0, vec(g_q_lora[l]), vec(g_kv_lora[l]),
                                               cos_t, sin_t)
        q_mla = q_up_proj(cqn, w_q, l, cos_t, sin_t)
        kv_mla = matmul(ckvn, w_kv, l, w_kv.shape[2], BF16, 1024, 1024, name="kv_up_proj")
        f_t = fox_decay(z_slab, b_rows[l], batch, seq)
        o_mla = mla_attention(q_mla, kv_mla, k_rot, batch, seq)
        o_swa = swa_attention(qkv_swa, pos_b, pos_row, slopes * LOG2_E, swa_sinks[l].astype(F32) * LOG2_E,
                              batch, seq)
        o_fox = fox_attention(qkv_fox, f_t, batch, seq)
        merged = gated_merge(o_mla, o_swa, o_fox, w_br, l, gates)
        xs, h2, r2 = proj_norm(merged, w_o, l, xs, vec(g_mix_post[l]), vec(g_ffn_pre[l]), 512, 1024)
        act = ffn_up(h2, r2, w_gu, l)
        g_next = vec(g_mix_pre[l + 1]) if l + 1 < depth else None
        xs, h, r = proj_norm(act, w_dn, l, xs, vec(g_ffn_post[l]), g_next, 512, 256)
    return xs.reshape(batch, seq, d)
```

```python
import functools
import math
from typing import Any, NamedTuple

import jax
import jax.numpy as jnp
from jax import lax
from jax.experimental import pallas as pl
from jax.experimental.pallas import tpu as pltpu

F32 = jnp.float32
BF16 = jnp.bfloat16

VMEM_LIMIT_BYTES = 56 * 1024 * 1024
LANES = 128

HEAD_DIM = 128
MLA_HEADS = 8
MLA_Q_LORA = 1024
MLA_KV_LORA = 512
MLA_NOPE = 128
MLA_ROPE = 64
MLA_QK = MLA_NOPE + MLA_ROPE
SWA_HEADS = 16
SWA_KV_HEADS = 2
SWA_GROUP = SWA_HEADS // SWA_KV_HEADS
WINDOW = 128
FOX_HEADS = 8
ROPE_THETA = 10000.0
RMS_EPS = 1e-6
LOG2_E = math.log2(math.e)
Q_TILE = 256
HEADS_PER_STEP = 2

SWA_WIDTH = (SWA_HEADS + 2 * SWA_KV_HEADS) * HEAD_DIM
FOX_WIDTH = 3 * FOX_HEADS * HEAD_DIM
LAT_WIDTH = MLA_Q_LORA + MLA_KV_LORA + 2 * LANES


def _params(*sem):
    return pltpu.CompilerParams(dimension_semantics=sem, vmem_limit_bytes=VMEM_LIMIT_BYTES)


def _inv_rms(x):
    return lax.rsqrt(jnp.mean(x * x, axis=-1, keepdims=True) + RMS_EPS)


def _dot(a, b):
    return jnp.dot(a, b, preferred_element_type=F32)


def _dot_nt(a, b):
    return lax.dot_general(a, b, (((1,), (1,)), ((), ())), preferred_element_type=F32)


def _sigmoid(x):
    return 1.0 / (1.0 + jnp.exp(-x))


def _row_chunks(rows, chunks):
    step = rows // chunks
    return [slice(c * step, (c + 1) * step) for c in range(chunks)]


def _rope_table_kernel(pos_ref, c_ref, cos_ref, sin_ref):
    ang = pos_ref[...].astype(F32) * c_ref[0:1, :]
    cos_ref[...] = jnp.cos(ang) * c_ref[1:2, :]
    sin_ref[...] = jnp.sin(ang) * c_ref[2:3, :]


def rope_tables(pos_b, consts, tm=2048):
    m = pos_b.shape[0]
    spec = pl.BlockSpec((tm, LANES), lambda i: (i, 0))
    return pl.pallas_call(
        _rope_table_kernel,
        grid=(m // tm,),
        in_specs=[spec, pl.BlockSpec((8, LANES), lambda i: (0, 0))],
        out_specs=[spec, spec],
        out_shape=[jax.ShapeDtypeStruct((m, LANES), F32)] * 2,
        compiler_params=_params("parallel"),
        name="rope_tables",
    )(pos_b, consts)


def _prenorm_kernel(x_ref, g_ref, h_ref, r_ref):
    x = x_ref[...]
    h_ref[...] = (x * g_ref[...]).astype(h_ref.dtype)
    r_ref[...] = jnp.broadcast_to(_inv_rms(x), r_ref.shape)


def prenorm(x, gain, tm=256):
    m, d = x.shape
    return pl.pallas_call(
        _prenorm_kernel,
        grid=(m // tm,),
        in_specs=[pl.BlockSpec((tm, d), lambda i: (i, 0)), pl.BlockSpec((1, d), lambda i: (0, 0))],
        out_specs=[pl.BlockSpec((tm, d), lambda i: (i, 0)), pl.BlockSpec((tm, LANES), lambda i: (i, 0))],
        out_shape=[jax.ShapeDtypeStruct((m, d), BF16), jax.ShapeDtypeStruct((m, LANES), F32)],
        compiler_params=_params("parallel"),
        name="prenorm",
    )(x, gain)


class SideCast(NamedTuple):
    src: Any
    layer: int


BF16_SUBLANES = 16


def _side_cast_specs(side, grid):
    _, rows, cols = side.src.shape
    steps = math.prod(grid)
    n_blocks = max(nb for nb in range(1, steps + 1) if rows % (nb * BF16_SUBLANES) == 0)
    br = rows // n_blocks

    def row_block(*ids):
        step = 0
        for extent, idx in zip(grid, ids):
            step = step * extent + idx
        return jnp.minimum(step, n_blocks - 1)

    in_spec = pl.BlockSpec((None, br, cols), lambda *ids: (side.layer, row_block(*ids), 0))
    out_spec = pl.BlockSpec((None, br, cols), lambda *ids: (0, row_block(*ids), 0))
    return in_spec, out_spec, jax.ShapeDtypeStruct((1, rows, cols), BF16)


def _matmul_kernel(*refs, act, row_scaled, col_scaled, side, chunks):
    refs = list(refs)
    a_ref = refs.pop(0)
    r_ref = refs.pop(0) if row_scaled else None
    c_ref = refs.pop(0) if col_scaled else None
    w_ref = refs.pop(0)
    if side:
        src_ref, o_ref, dst_ref = refs
        dst_ref[...] = src_ref[...].astype(dst_ref.dtype)
    else:
        (o_ref,) = refs
    for rs in _row_chunks(a_ref.shape[0], chunks):
        acc = _dot(a_ref[rs, :], w_ref[...])
        if row_scaled:
            acc = acc * r_ref[rs, 0:1]
        if col_scaled:
            acc = acc * c_ref[...]
        if act == "sigmoid":
            acc = _sigmoid(acc)
        o_ref[rs, :] = acc.astype(o_ref.dtype)


def matmul(a, w, layer, n, out_dtype, tm, tn, row_scale=None, col_scale=None, act=None, side=None, chunks=1,
           name="matmul"):
    m, k = a.shape
    assert m % tm == 0 and n % tn == 0 and w.shape[1:] == (k, n)
    grid = (m // tm, n // tn)
    in_specs = [pl.BlockSpec((tm, k), lambda i, j: (i, 0))]
    args = [a]
    if row_scale is not None:
        in_specs.append(pl.BlockSpec((tm, LANES), lambda i, j: (i, 0)))
        args.append(row_scale)
    if col_scale is not None:
        in_specs.append(pl.BlockSpec((1, tn), lambda i, j: (0, j)))
        args.append(col_scale)
    in_specs.append(pl.BlockSpec((None, k, tn), lambda i, j: (layer, 0, j)))
    args.append(w)
    out_specs = [pl.BlockSpec((tm, tn), lambda i, j: (i, j))]
    out_shape = [jax.ShapeDtypeStruct((m, n), out_dtype)]
    if side is not None:
        side_in, side_out, side_shape = _side_cast_specs(side, grid)
        in_specs.append(side_in)
        args.append(side.src)
        out_specs.append(side_out)
        out_shape.append(side_shape)
    res = pl.pallas_call(
        functools.partial(_matmul_kernel, act=act, row_scaled=row_scale is not None,
                          col_scaled=col_scale is not None, side=side is not None, chunks=chunks),
        grid=grid,
        in_specs=in_specs,
        out_specs=out_specs,
        out_shape=out_shape,
        compiler_params=_params("arbitrary", "arbitrary"),
        name=name,
    )(*args)
    return res if side is not None else res[0]


def _proj_norm_kernel(*refs, n_i, n_j, d, emit_next):
    if emit_next:
        a_ref, w_ref, x_ref, gp_ref, gn_ref, xo_ref, h_ref, r_ref, y_scr, acc_scr, done_scr, acc2_scr = refs
    else:
        a_ref, w_ref, x_ref, gp_ref, xo_ref, y_scr, acc_scr, done_scr = refs
    i = pl.program_id(0)
    j = pl.program_id(1)

    @pl.when((i == 0) & (j == 0))
    def _():
        y_scr[...] = jnp.zeros_like(y_scr)
        acc_scr[...] = jnp.zeros_like(acc_scr)
        done_scr[...] = jnp.zeros_like(done_scr)
        if emit_next:
            acc2_scr[...] = jnp.zeros_like(acc2_scr)

    first = j == 0
    y_prev = y_scr[j]
    y_new = _dot(a_ref[...], w_ref[...])
    y_scr[j] = y_new
    part = jnp.sum(y_new * y_new, axis=-1, keepdims=True)
    acc_old = acc_scr[...]
    done = jnp.where(first, acc_old, done_scr[...])
    done_scr[...] = done
    acc_scr[...] = jnp.where(first, part, acc_old + part)

    x_new = x_ref[...] + y_prev * lax.rsqrt(done * (1.0 / d) + RMS_EPS) * gp_ref[...]
    xo_ref[...] = x_new
    if emit_next:
        h_ref[...] = (x_new * gn_ref[...]).astype(h_ref.dtype)
        part2 = jnp.sum(x_new * x_new, axis=-1, keepdims=True)
        acc2 = jnp.where(first, part2, acc2_scr[...] + part2)
        acc2_scr[...] = acc2
        r_ref[...] = jnp.broadcast_to(lax.rsqrt(acc2 * (1.0 / d) + RMS_EPS), r_ref.shape)


def proj_norm(a, w, layer, x, g_post, g_next, tm, tn):
    m, k = a.shape
    d = w.shape[2]
    n_i, n_j = m // tm, d // tn
    emit_next = g_next is not None
    prev = lambda i: jnp.maximum(i - 1, 0)
    col = pl.BlockSpec((tm, tn), lambda i, j: (prev(i), j))
    col_out = pl.BlockSpec((tm, tn), lambda i, j: (prev(i), jnp.where(i == 0, 0, j)))
    vec = pl.BlockSpec((1, tn), lambda i, j: (0, j))
    in_specs = [pl.BlockSpec((tm, k), lambda i, j: (jnp.minimum(i, n_i - 1), 0)),
                pl.BlockSpec((None, k, tn), lambda i, j: (layer, 0, j)),
                col, vec]
    args = [a, w, x, g_post]
    out_specs = [col_out]
    out_shape = [jax.ShapeDtypeStruct((m, d), F32)]
    scratch = [pltpu.VMEM((n_j, tm, tn), F32), pltpu.VMEM((tm, 1), F32), pltpu.VMEM((tm, 1), F32)]
    if emit_next:
        in_specs.append(vec)
        args.append(g_next)
        out_specs += [col_out, pl.BlockSpec((tm, LANES), lambda i, j: (prev(i), 0))]
        out_shape += [jax.ShapeDtypeStruct((m, d), BF16), jax.ShapeDtypeStruct((m, LANES), F32)]
        scratch.append(pltpu.VMEM((tm, 1), F32))
    res = pl.pallas_call(
        functools.partial(_proj_norm_kernel, n_i=n_i, n_j=n_j, d=d, emit_next=emit_next),
        grid=(n_i + 1, n_j),
        in_specs=in_specs,
        out_specs=out_specs,
        out_shape=out_shape,
        scratch_shapes=scratch,
        compiler_params=_params("arbitrary", "arbitrary"),
        name="proj_norm" if emit_next else "proj_norm_last",
    )(*args)
    return res if emit_next else (res[0], None, None)


def _latent_kernel(h_ref, r_ref, w_ref, gq_ref, gkv_ref, cos_ref, sin_ref, src_ref,
                   cq_ref, ckv_ref, kr_ref, z_ref, dst_ref, *, chunks):
    dst_ref[...] = src_ref[...].astype(dst_ref.dtype)
    q_end = MLA_Q_LORA
    kv_end = q_end + MLA_KV_LORA
    for rs in _row_chunks(h_ref.shape[0], chunks):
        acc = _dot(h_ref[rs, :], w_ref[...]) * r_ref[rs, 0:1]
        cq = acc[:, :q_end]
        ckv = acc[:, q_end:kv_end]
        cq_ref[rs, :] = (cq * _inv_rms(cq) * gq_ref[...]).astype(cq_ref.dtype)
        ckv_ref[rs, :] = (ckv * _inv_rms(ckv) * gkv_ref[...]).astype(ckv_ref.dtype)
        slab = acc[:, kv_end:kv_end + LANES]
        rot = slab * cos_ref[rs, :] + pltpu.roll(slab, LANES // 2, 1) * sin_ref[rs, :]
        kr_ref[rs, :] = rot.astype(kr_ref.dtype)
        z_ref[rs, :] = acc[:, kv_end + LANES:kv_end + 2 * LANES]


def latent_proj(h, r, w_lat, layer, g_q, g_kv, cos_t, sin_t, side, tm=512, chunks=2):
    m, d = h.shape
    n = LAT_WIDTH
    grid = (m // tm,)
    row = lambda width: pl.BlockSpec((tm, width), lambda i: (i, 0))
    const = lambda c: pl.BlockSpec((1, c), lambda i: (0, 0))
    side_in, side_out, side_shape = _side_cast_specs(side, grid)
    return pl.pallas_call(
        functools.partial(_latent_kernel, chunks=chunks),
        grid=grid,
        in_specs=[row(d), row(LANES), pl.BlockSpec((None, d, n), lambda i: (layer, 0, 0)),
                  const(MLA_Q_LORA), const(MLA_KV_LORA), row(LANES), row(LANES), side_in],
        out_specs=[row(MLA_Q_LORA), row(MLA_KV_LORA), row(LANES), row(LANES), side_out],
        out_shape=[jax.ShapeDtypeStruct((m, MLA_Q_LORA), BF16), jax.ShapeDtypeStruct((m, MLA_KV_LORA), BF16),
                   jax.ShapeDtypeStruct((m, LANES), BF16), jax.ShapeDtypeStruct((m, LANES), F32), side_shape],
        compiler_params=_params("arbitrary"),
        name="latent_proj",
    )(h, r, w_lat, g_q, g_kv, cos_t, sin_t, side.src)


def _q_up_kernel(a_ref, w_ref, cos_ref, sin_ref, o_ref, *, scale):
    acc = _dot(a_ref[...], w_ref[...])
    c = cos_ref[...]
    s = sin_ref[...]
    for h in range(MLA_HEADS):
        base = 2 * LANES * h
        nope = acc[:, base:base + LANES]
        slab = acc[:, base + LANES:base + 2 * LANES]
        rot = slab * c + pltpu.roll(slab, LANES // 2, 1) * s
        o_ref[:, base:base + LANES] = (nope * scale).astype(o_ref.dtype)
        o_ref[:, base + LANES:base + 2 * LANES] = (rot * scale).astype(o_ref.dtype)


def q_up_proj(cqn, w_uq, layer, cos_t, sin_t, tm=1024):
    m, k = cqn.shape
    n = w_uq.shape[2]
    return pl.pallas_call(
        functools.partial(_q_up_kernel, scale=MLA_QK ** -0.5 * LOG2_E),
        grid=(m // tm,),
        in_specs=[pl.BlockSpec((tm, k), lambda i: (i, 0)), pl.BlockSpec((None, k, n), lambda i: (layer, 0, 0)),
                  pl.BlockSpec((tm, LANES), lambda i: (i, 0)), pl.BlockSpec((tm, LANES), lambda i: (i, 0))],
        out_specs=pl.BlockSpec((tm, n), lambda i: (i, 0)),
        out_shape=jax.ShapeDtypeStruct((m, n), BF16),
        compiler_params=_params("parallel"),
        name="q_up_proj",
    )(cqn, w_uq, cos_t, sin_t)


class _Head(NamedTuple):
    q: Any
    q_cols: slice
    k: Any
    k_cols: slice
    v: Any
    v_cols: slice
    o: Any
    o_cols: slice
    fq: Any = None
    fk_row: Any = None


def _causal_blocks(heads, seq, tq):
    row = lax.broadcasted_iota(jnp.int32, (tq, tq), 0)
    col = lax.broadcasted_iota(jnp.int32, (tq, tq), 1)
    tri = col <= row
    for i in range(seq // tq):
        lo, hi = i * tq, (i + 1) * tq
        for hd in heads:
            qi = hd.q[lo:hi, hd.q_cols]
            s_d = _dot_nt(qi, hd.k[lo:hi, hd.k_cols])
            if hd.fk_row is not None:
                s_d = s_d - hd.fk_row[:, lo:hi]
            s_d = jnp.where(tri, s_d, -jnp.inf)
            mx = jnp.max(s_d, axis=-1, keepdims=True)
            if i > 0:
                s_o = _dot_nt(qi, hd.k[:lo, hd.k_cols])
                if hd.fk_row is not None:
                    s_o = s_o - hd.fk_row[:, :lo]
                mx = jnp.maximum(mx, jnp.max(s_o, axis=-1, keepdims=True))
            if hd.fq is not None:
                fq = hd.fq[lo:hi, 0:1]
                off = fq - (mx + fq)
            else:
                off = -mx
            p_d = jnp.exp2(s_d + off)
            den = jnp.sum(p_d, axis=-1, keepdims=True)
            acc = _dot(p_d.astype(BF16), hd.v[lo:hi, hd.v_cols])
            if i > 0:
                p_o = jnp.exp2(s_o + off)
                den = den + jnp.sum(p_o, axis=-1, keepdims=True)
                acc = acc + _dot(p_o.astype(BF16), hd.v[:lo, hd.v_cols])
            hd.o[lo:hi, hd.o_cols] = (acc / den).astype(hd.o.dtype)


def _mla_attn_kernel(q_ref, kv_ref, kr_ref, o_ref, k_scr, *, seq, tq, hps):
    heads = []
    for u in range(hps):
        base = 2 * LANES * u
        k_scr[u, :, :LANES] = kv_ref[:, base:base + LANES]
        k_scr[u, :, LANES:] = kr_ref[...]
        heads.append(_Head(q_ref, slice(base, base + 2 * LANES), k_scr.at[u], slice(None),
                           kv_ref, slice(base + LANES, base + 2 * LANES), o_ref, slice(LANES * u, LANES * (u + 1))))
    _causal_blocks(heads, seq, tq)


def mla_attention(q, kv, k_rot, batch, seq, hps=HEADS_PER_STEP):
    m = q.shape[0]
    return pl.pallas_call(
        functools.partial(_mla_attn_kernel, seq=seq, tq=Q_TILE, hps=hps),
        grid=(batch, MLA_HEADS // hps),
        in_specs=[pl.BlockSpec((seq, 2 * LANES * hps), lambda b, g: (b, g)),
                  pl.BlockSpec((seq, 2 * LANES * hps), lambda b, g: (b, g)),
                  pl.BlockSpec((seq, LANES), lambda b, g: (b, 0))],
        out_specs=pl.BlockSpec((seq, LANES * hps), lambda b, g: (b, g)),
        out_shape=jax.ShapeDtypeStruct((m, MLA_HEADS * LANES), BF16),
        scratch_shapes=[pltpu.VMEM((hps, seq, 2 * LANES), BF16)],
        compiler_params=_params("parallel", "arbitrary"),
        name="mla_attention",
    )(q, kv, k_rot)


def _fox_attn_kernel(q_ref, k_ref, v_ref, ft_ref, o_ref, fq_scr, *, seq, tq, hps):
    g = pl.program_id(1)
    heads = []
    for u in range(hps):
        fk_row = ft_ref[0, pl.ds(g * hps + u, 1), :]
        fq_scr[u] = jnp.transpose(jnp.broadcast_to(fk_row, (LANES, seq)))
        cols = slice(LANES * u, LANES * (u + 1))
        heads.append(_Head(q_ref, cols, k_ref, cols, v_ref, cols, o_ref, cols, fq_scr.at[u], fk_row))
    _causal_blocks(heads, seq, tq)


def fox_attention(qkv, f_t, batch, seq, hps=HEADS_PER_STEP):
    m = qkv.shape[0]
    nh = FOX_HEADS
    ng = nh // hps
    blk = lambda part: pl.BlockSpec((seq, LANES * hps), lambda b, g: (b, part * ng + g))
    return pl.pallas_call(
        functools.partial(_fox_attn_kernel, seq=seq, tq=Q_TILE, hps=hps),
        grid=(batch, ng),
        in_specs=[blk(0), blk(1), blk(2), pl.BlockSpec((1, nh, seq), lambda b, g: (b, 0, 0))],
        out_specs=pl.BlockSpec((seq, LANES * hps), lambda b, g: (b, g)),
        out_shape=jax.ShapeDtypeStruct((m, nh * LANES), BF16),
        scratch_shapes=[pltpu.VMEM((hps, seq, LANES), F32)],
        compiler_params=_params("parallel", "arbitrary"),
        name="fox_attention",
    )(qkv, qkv, qkv, f_t)


def _fox_decay_kernel(z_ref, b_ref, ft_ref, *, seq):
    x = z_ref[...] + b_ref[...]
    log_f = jnp.minimum(x, 0.0) - jnp.log1p(jnp.exp(-jnp.abs(x)))
    t = jnp.transpose(log_f)[0:FOX_HEADS, :]
    lane = lax.broadcasted_iota(jnp.int32, t.shape, 1)
    shift = 1
    while shift < seq:
        t = t + jnp.where(lane >= shift, pltpu.roll(t, shift, 1), 0.0)
        shift *= 2
    ft_ref[0] = t * LOG2_E


def fox_decay(z_slab, b_row, batch, seq):
    return pl.pallas_call(
        functools.partial(_fox_decay_kernel, seq=seq),
        grid=(batch,),
        in_specs=[pl.BlockSpec((seq, LANES), lambda b: (b, 0)), pl.BlockSpec((1, LANES), lambda b: (0, 0))],
        out_specs=pl.BlockSpec((1, FOX_HEADS, seq), lambda b: (b, 0, 0)),
        out_shape=jax.ShapeDtypeStruct((batch, FOX_HEADS, seq), F32),
        compiler_params=_params("parallel"),
        name="fox_decay",
    )(z_slab, b_row)


def _swa_kernel(slope_ref, sink_ref, q_ref, kp_ref, kc_ref, vp_ref, vc_ref, qpos_ref, pp_ref, pc_ref, o_ref):
    n = pl.program_id(1)
    w = WINDOW
    i = lax.broadcasted_iota(jnp.int32, (w, w), 0)
    j = lax.broadcasted_iota(jnp.int32, (w, w), 1)
    valid = jnp.concatenate([(j > i) & (n > 0), j <= i], axis=1)
    qpos = qpos_ref[:, 0:1]
    dist = jnp.concatenate([qpos - pp_ref[0], qpos - pc_ref[0]], axis=1).astype(F32)
    neg_dist = jnp.where(valid, -dist, -jnp.inf)
    for kh in range(SWA_KV_HEADS):
        ks = slice(kh * HEAD_DIM, (kh + 1) * HEAD_DIM)
        k_win = jnp.concatenate([kp_ref[:, ks], kc_ref[:, ks]], axis=0)
        v_win = jnp.concatenate([vp_ref[:, ks], vc_ref[:, ks]], axis=0)
        for g in range(SWA_GROUP):
            hd = kh * SWA_GROUP + g
            hs = slice(hd * HEAD_DIM, (hd + 1) * HEAD_DIM)
            s = _dot_nt(q_ref[:, hs], k_win) + slope_ref[hd] * neg_dist
            sink = sink_ref[hd]
            mx = jnp.maximum(jnp.max(s, axis=-1, keepdims=True), sink)
            e = jnp.exp2(s - mx)
            den = jnp.sum(e, axis=-1, keepdims=True) + jnp.exp2(sink - mx)
            o_ref[:, hs] = (_dot(e.astype(BF16), v_win) / den).astype(o_ref.dtype)


def swa_attention(qkv, pos_b, pos_row, slopes, sinks, batch, seq):
    m = qkv.shape[0]
    w = WINDOW
    nb = seq // w
    qw = SWA_HEADS * HEAD_DIM
    kvw = SWA_KV_HEADS * HEAD_DIM
    cur = lambda b, n: b * nb + n
    prev = lambda b, n: b * nb + jnp.maximum(n - 1, 0)
    smem = pl.BlockSpec(memory_space=pltpu.SMEM)
    return pl.pallas_call(
        _swa_kernel,
        grid=(batch, nb),
        in_specs=[smem, smem,
                  pl.BlockSpec((w, qw), lambda b, n: (cur(b, n), 0)),
                  pl.BlockSpec((w, kvw), lambda b, n: (prev(b, n), qw // kvw)),
                  pl.BlockSpec((w, kvw), lambda b, n: (cur(b, n), qw // kvw)),
                  pl.BlockSpec((w, kvw), lambda b, n: (prev(b, n), qw // kvw + 1)),
                  pl.BlockSpec((w, kvw), lambda b, n: (cur(b, n), qw // kvw + 1)),
                  pl.BlockSpec((w, LANES), lambda b, n: (cur(b, n), 0)),
                  pl.BlockSpec((1, 1, w), lambda b, n: (prev(b, n), 0, 0)),
                  pl.BlockSpec((1, 1, w), lambda b, n: (cur(b, n), 0, 0))],
        out_specs=pl.BlockSpec((w, qw), lambda b, n: (cur(b, n), 0)),
        out_shape=jax.ShapeDtypeStruct((m, qw), BF16),
        compiler_params=_params("parallel", "arbitrary"),
        name="swa_attention",
    )(slopes, sinks, qkv, qkv, qkv, qkv, qkv, pos_b, pos_row, pos_row)


def _merge_kernel(mla_ref, swa_lo_ref, swa_hi_ref, fox_ref, w0_ref, w1_ref, w2_ref, w3_ref,
                  g0_ref, g1_ref, g2_ref, o_ref, *, chunks):
    for rs in _row_chunks(o_ref.shape[0], chunks):
        acc = g0_ref[rs, :].astype(F32) * _dot(mla_ref[rs, :], w0_ref[...])
        swa = _dot(swa_lo_ref[rs, :], w1_ref[...]) + _dot(swa_hi_ref[rs, :], w2_ref[...])
        acc = acc + g1_ref[rs, :].astype(F32) * swa
        acc = acc + g2_ref[rs, :].astype(F32) * _dot(fox_ref[rs, :], w3_ref[...])
        o_ref[rs, :] = acc.astype(o_ref.dtype)


def gated_merge(o_mla, o_swa, o_fox, w_branch, layer, gates, tm=1024, tn=512, chunks=2):
    m, kb = o_mla.shape
    d = w_branch.shape[2]
    nj = d // tn
    assert o_swa.shape[1] == 2 * kb and o_fox.shape[1] == kb and w_branch.shape[1] == 4 * kb
    a_spec = lambda cb: pl.BlockSpec((tm, kb), lambda i, j: (i, cb))
    w_spec = lambda rb: pl.BlockSpec((None, kb, tn), lambda i, j: (layer, rb, j))
    g_spec = lambda br: pl.BlockSpec((tm, tn), lambda i, j: (i, br * nj + j))
    return pl.pallas_call(
        functools.partial(_merge_kernel, chunks=chunks),
        grid=(m // tm, nj),
        in_specs=[a_spec(0), a_spec(0), a_spec(1), a_spec(0)] + [w_spec(rb) for rb in range(4)]
                 + [g_spec(br) for br in range(3)],
        out_specs=pl.BlockSpec((tm, tn), lambda i, j: (i, j)),
        out_shape=jax.ShapeDtypeStruct((m, d), BF16),
        compiler_params=_params("parallel", "arbitrary"),
        name="gated_merge",
    )(o_mla, o_swa, o_swa, o_fox, w_branch, w_branch, w_branch, w_branch, gates, gates, gates)


def _ffn_up_kernel(h_ref, r_ref, wg_ref, wu_ref, src_ref, o_ref, dst_ref, *, chunks):
    dst_ref[...] = src_ref[...].astype(dst_ref.dtype)
    for rs in _row_chunks(h_ref.shape[0], chunks):
        r = r_ref[rs, 0:1]
        gate = _dot(h_ref[rs, :], wg_ref[...]) * r
        up = _dot(h_ref[rs, :], wu_ref[...]) * r
        o_ref[rs, :] = (gate * _sigmoid(gate) * up).astype(o_ref.dtype)


def ffn_up(h, r, w_gate_up, layer, side, tm=2048, tn=256, chunks=4):
    m, d = h.shape
    n = w_gate_up.shape[2] // 2
    assert n % tn == 0
    nj = n // tn
    grid = (m // tm, nj)
    side_in, side_out, side_shape = _side_cast_specs(side, grid)
    return pl.pallas_call(
        functools.partial(_ffn_up_kernel, chunks=chunks),
        grid=grid,
        in_specs=[pl.BlockSpec((tm, d), lambda i, j: (i, 0)),
                  pl.BlockSpec((tm, LANES), lambda i, j: (i, 0)),
                  pl.BlockSpec((None, d, tn), lambda i, j: (layer, 0, j)),
                  pl.BlockSpec((None, d, tn), lambda i, j: (layer, 0, nj + j)),
                  side_in],
        out_specs=[pl.BlockSpec((tm, tn), lambda i, j: (i, j)), side_out],
        out_shape=[jax.ShapeDtypeStruct((m, n), BF16), side_shape],
        compiler_params=_params("arbitrary", "arbitrary"),
        name="ffn_up",
    )(h, r, w_gate_up, w_gate_up, side.src)


def _regroup_kernel(main_ref, next_ref, o_ref, *, shift):
    x = jnp.concatenate([main_ref[shift:, :], next_ref[:shift, :]], axis=0)
    o_ref[...] = jnp.transpose(x).astype(o_ref.dtype)


def regroup_cast(w_t, row0, width, tile=512, tk=2048):
    depth, _, k = w_t.shape
    shift = row0 % LANES
    base = row0 - shift
    assert shift % 8 == 0 and shift and base % tile == 0 and width % tile == 0 and k % tk == 0
    return pl.pallas_call(
        functools.partial(_regroup_kernel, shift=shift),
        grid=(depth, k // tk, width // tile),
        in_specs=[pl.BlockSpec((None, tile, tk), lambda l, kb, c: (l, base // tile + c, kb)),
                  pl.BlockSpec((None, LANES, tk), lambda l, kb, c: (l, (base + (c + 1) * tile) // LANES, kb))],
        out_specs=pl.BlockSpec((None, tk, tile), lambda l, kb, c: (l, kb, c)),
        out_shape=jax.ShapeDtypeStruct((depth, k, width), BF16),
        compiler_params=_params("parallel", "parallel", "arbitrary"),
        name="regroup_cast",
    )(w_t, w_t)


def _latent_weight_kernel(main_ref, kr_ref, z_ref, o_ref):
    half = MLA_ROPE // 2
    kr = kr_ref[...]
    pad = jnp.zeros((LANES - FOX_HEADS, kr.shape[1]), kr.dtype)
    tail = jnp.concatenate([kr, kr[half:, :], kr[:half, :], z_ref[...], pad], axis=0)
    n_main = main_ref.shape[0]
    o_ref[:, :n_main] = jnp.transpose(main_ref[...]).astype(o_ref.dtype)
    o_ref[:, n_main:] = jnp.transpose(tail).astype(o_ref.dtype)


def latent_weights(w_t, kr_row, z_row, tk=1024):
    depth, _, k = w_t.shape
    n_main = kr_row
    assert n_main % LANES == 0 and kr_row % MLA_ROPE == 0 and z_row % FOX_HEADS == 0 and FOX_HEADS % 8 == 0
    return pl.pallas_call(
        _latent_weight_kernel,
        grid=(depth, k // tk),
        in_specs=[pl.BlockSpec((None, n_main, tk), lambda l, kb: (l, 0, kb)),
                  pl.BlockSpec((None, MLA_ROPE, tk), lambda l, kb: (l, kr_row // MLA_ROPE, kb)),
                  pl.BlockSpec((None, FOX_HEADS, tk), lambda l, kb: (l, z_row // FOX_HEADS, kb))],
        out_specs=pl.BlockSpec((None, tk, LAT_WIDTH), lambda l, kb: (l, kb, 0)),
        out_shape=jax.ShapeDtypeStruct((depth, k, LAT_WIDTH), BF16),
        compiler_params=_params("parallel", "parallel"),
        name="latent_weights",
    )(w_t, w_t, w_t)


def _input_proj_weights(w_in):
    d = w_in.shape[1]
    w_t = jnp.swapaxes(w_in, 1, 2)
    o = 0
    cuts = {}
    for name, width in (("cq", MLA_Q_LORA), ("ckv", MLA_KV_LORA), ("kr", MLA_ROPE), ("swa", SWA_WIDTH),
                        ("fox", FOX_WIDTH), ("z", FOX_HEADS), ("gate", 3 * d)):
        cuts[name] = (o, o + width)
        o += width
    assert cuts["cq"][0] == 0 and cuts["kr"][0] == MLA_Q_LORA + MLA_KV_LORA
    w_lat = latent_weights(w_t, cuts["kr"][0], cuts["z"][0])
    w_gate = regroup_cast(w_t, cuts["gate"][0], 3 * d)
    w_fox = regroup_cast(w_t, cuts["fox"][0], FOX_WIDTH)
    w_swa = regroup_cast(w_t, cuts["swa"][0], SWA_WIDTH)
    return w_gate, w_fox, w_swa, w_lat


def _q_up_weights(w_uq):
    depth = w_uq.shape[0]
    half = MLA_ROPE // 2
    uq = w_uq.reshape(depth, MLA_Q_LORA, MLA_HEADS, MLA_QK)
    rope = uq[..., MLA_NOPE:]
    rope_swapped = jnp.concatenate([rope[..., half:], rope[..., :half]], axis=-1)
    w_q = jnp.concatenate([uq[..., :MLA_NOPE], rope, rope_swapped], axis=-1)
    return w_q.reshape(depth, MLA_Q_LORA, MLA_HEADS * 2 * LANES).astype(BF16)


def _rope_consts():
    half = MLA_ROPE // 2
    inv_freq = ROPE_THETA ** (-jnp.arange(half, dtype=F32) / half)
    zeros = jnp.zeros((LANES - MLA_ROPE,), F32)
    ones = jnp.ones((half,), F32)
    rows = jnp.stack([jnp.concatenate([inv_freq, inv_freq, zeros]),
                      jnp.concatenate([ones, ones, zeros]),
                      jnp.concatenate([-ones, ones, zeros])])
    return jnp.concatenate([rows, jnp.zeros((5, LANES), F32)], axis=0)


def kernel(x, positions, g_mix_pre, g_mix_post, g_ffn_pre, g_ffn_post, w_in, g_q_lora, w_uq, g_kv_lora, w_ukv,
           b_forget, swa_sinks, w_branch, w_out, w_gate_up, w_down):
    batch, seq, d = x.shape
    depth = w_in.shape[0]
    m = batch * seq
    xs = x.reshape(m, d)
    pos_b = jnp.broadcast_to(positions.reshape(m, 1), (m, LANES))
    pos_row = positions.reshape(m // WINDOW, 1, WINDOW)
    slopes = 2.0 ** (-8.0 * jnp.arange(1, SWA_HEADS + 1, dtype=F32) / SWA_HEADS)
    cos_t, sin_t = rope_tables(pos_b, _rope_consts())
    vec = lambda g: g.reshape(1, -1)

    w_gate, w_fox, w_swa, w_lat = _input_proj_weights(w_in)
    w_q = _q_up_weights(w_uq)
    w_kv = w_ukv.astype(BF16)
    b_rows = jnp.zeros((depth, 1, LANES), F32).at[:, 0, :FOX_HEADS].set(b_forget)
    q_scale = HEAD_DIM ** -0.5 * LOG2_E
    qkv_scale = lambda q_width, width: jnp.concatenate(
        [jnp.full((1, q_width), q_scale, F32), jnp.ones((1, width - q_width), F32)], axis=1)
    fox_scale = qkv_scale(FOX_HEADS * HEAD_DIM, FOX_WIDTH)
    swa_scale = qkv_scale(SWA_HEADS * HEAD_DIM, SWA_WIDTH)

    h, r = prenorm(xs, vec(g_mix_pre[0]))
    for l in range(depth):
        gates, w_gu = matmul(h, w_gate, l, 3 * d, BF16, 1024, 1024, row_scale=r, act="sigmoid", chunks=2,
                             side=SideCast(w_gate_up, l), name="gate_proj")
        qkv_fox, w_br = matmul(h, w_fox, l, FOX_WIDTH, BF16, 1024, 1024, row_scale=r, col_scale=fox_scale,
                               chunks=2, side=SideCast(w_branch, l), name="fox_proj")
        qkv_swa = matmul(h, w_swa, l, SWA_WIDTH, BF16, 1024, SWA_WIDTH // 2, row_scale=r, col_scale=swa_scale,
                         chunks=2, name="swa_proj")
        cqn, ckvn, k_rot, z_slab, w_o = latent_proj(h, r, w_lat, l, vec(g_q_lora[l]), vec(g_kv_lora[l]),
                                                    cos_t, sin_t, SideCast(w_out, l))
        q_mla = q_up_proj(cqn, w_q, l, cos_t, sin_t)
        kv_mla = matmul(ckvn, w_kv, l, w_kv.shape[2], BF16, 1024, 1024, name="kv_up_proj")
        f_t = fox_decay(z_slab, b_rows[l], batch, seq)
        o_mla = mla_attention(q_mla, kv_mla, k_rot, batch, seq)
        o_swa = swa_attention(qkv_swa, pos_b, pos_row, slopes * LOG2_E, swa_sinks[l].astype(F32) * LOG2_E,
                              batch, seq)
        o_fox = fox_attention(qkv_fox, f_t, batch, seq)
        merged = gated_merge(o_mla, o_swa, o_fox, w_br, 0, gates)
        xs, h2, r2 = proj_norm(merged, w_o, 0, xs, vec(g_mix_post[l]), vec(g_ffn_pre[l]), 512, 1024)
        act, w_dn = ffn_up(h2, r2, w_gu, 0, SideCast(w_down, l))
        g_next = vec(g_mix_pre[l + 1]) if l + 1 < depth else None
        xs, h, r = proj_norm(act, w_dn, 0, xs, vec(g_ffn_post[l]), g_next, 512, 256)
    return xs.reshape(batch, seq, d)
```

```python
import functools
import math
from typing import Any, NamedTuple

import jax
import jax.numpy as jnp
from jax import lax
from jax.experimental import pallas as pl
from jax.experimental.pallas import tpu as pltpu

F32 = jnp.float32
BF16 = jnp.bfloat16

VMEM_LIMIT_BYTES = 56 * 1024 * 1024
LANES = 128

HEAD_DIM = 128
MLA_HEADS = 8
MLA_Q_LORA = 1024
MLA_KV_LORA = 512
MLA_NOPE = 128
MLA_ROPE = 64
MLA_QK = MLA_NOPE + MLA_ROPE
SWA_HEADS = 16
SWA_KV_HEADS = 2
SWA_GROUP = SWA_HEADS // SWA_KV_HEADS
WINDOW = 128
FOX_HEADS = 8
ROPE_THETA = 10000.0
RMS_EPS = 1e-6
LOG2_E = math.log2(math.e)
Q_TILE = 512
HEADS_PER_STEP = 2

SWA_WIDTH = (SWA_HEADS + 2 * SWA_KV_HEADS) * HEAD_DIM
FOX_WIDTH = 3 * FOX_HEADS * HEAD_DIM
LAT_WIDTH = MLA_Q_LORA + MLA_KV_LORA + 2 * LANES


def _params(*sem):
    return pltpu.CompilerParams(dimension_semantics=sem, vmem_limit_bytes=VMEM_LIMIT_BYTES)


def _inv_rms(x):
    return lax.rsqrt(jnp.mean(x * x, axis=-1, keepdims=True) + RMS_EPS)


def _dot(a, b):
    return jnp.dot(a, b, preferred_element_type=F32)


def _dot_nt(a, b):
    return lax.dot_general(a, b, (((1,), (1,)), ((), ())), preferred_element_type=F32)


def _sigmoid(x):
    return 0.5 * jnp.tanh(0.5 * x) + 0.5


def _row_chunks(rows, chunks):
    step = rows // chunks
    return [slice(c * step, (c + 1) * step) for c in range(chunks)]


def _rope_table_kernel(pos_ref, c_ref, cos_ref, sin_ref):
    ang = pos_ref[...].astype(F32) * c_ref[0:1, :]
    cos_ref[...] = jnp.cos(ang) * c_ref[1:2, :]
    sin_ref[...] = jnp.sin(ang) * c_ref[2:3, :]


def rope_tables(pos_b, consts, tm=2048):
    m = pos_b.shape[0]
    spec = pl.BlockSpec((tm, LANES), lambda i: (i, 0))
    return pl.pallas_call(
        _rope_table_kernel,
        grid=(m // tm,),
        in_specs=[spec, pl.BlockSpec((8, LANES), lambda i: (0, 0))],
        out_specs=[spec, spec],
        out_shape=[jax.ShapeDtypeStruct((m, LANES), F32)] * 2,
        compiler_params=_params("parallel"),
        name="rope_tables",
    )(pos_b, consts)


def _prenorm_kernel(x_ref, g_ref, h_ref, r_ref):
    x = x_ref[...]
    h_ref[...] = (x * g_ref[...]).astype(h_ref.dtype)
    r_ref[...] = jnp.broadcast_to(_inv_rms(x), r_ref.shape)


def prenorm(x, gain, tm=256):
    m, d = x.shape
    return pl.pallas_call(
        _prenorm_kernel,
        grid=(m // tm,),
        in_specs=[pl.BlockSpec((tm, d), lambda i: (i, 0)), pl.BlockSpec((1, d), lambda i: (0, 0))],
        out_specs=[pl.BlockSpec((tm, d), lambda i: (i, 0)), pl.BlockSpec((tm, LANES), lambda i: (i, 0))],
        out_shape=[jax.ShapeDtypeStruct((m, d), BF16), jax.ShapeDtypeStruct((m, LANES), F32)],
        compiler_params=_params("parallel"),
        name="prenorm",
    )(x, gain)


class SideCast(NamedTuple):
    src: Any
    layer: int


BF16_SUBLANES = 16


def _side_cast_specs(side, grid):
    _, rows, cols = side.src.shape
    steps = math.prod(grid)
    n_blocks = max(nb for nb in range(1, steps + 1) if rows % (nb * BF16_SUBLANES) == 0)
    br = rows // n_blocks

    def row_block(*ids):
        step = 0
        for extent, idx in zip(grid, ids):
            step = step * extent + idx
        return jnp.minimum(step, n_blocks - 1)

    in_spec = pl.BlockSpec((None, br, cols), lambda *ids: (side.layer, row_block(*ids), 0))
    out_spec = pl.BlockSpec((None, br, cols), lambda *ids: (0, row_block(*ids), 0))
    return in_spec, out_spec, jax.ShapeDtypeStruct((1, rows, cols), BF16)


def _matmul_kernel(*refs, act, row_scaled, col_scaled, side, chunks):
    refs = list(refs)
    a_ref = refs.pop(0)
    r_ref = refs.pop(0) if row_scaled else None
    c_ref = refs.pop(0) if col_scaled else None
    w_ref = refs.pop(0)
    if side:
        src_ref, o_ref, dst_ref = refs
        dst_ref[...] = src_ref[...].astype(dst_ref.dtype)
    else:
        (o_ref,) = refs
    for rs in _row_chunks(a_ref.shape[0], chunks):
        acc = _dot(a_ref[rs, :], w_ref[...])
        if row_scaled:
            acc = acc * r_ref[rs, 0:1]
        if col_scaled:
            acc = acc * c_ref[...]
        if act == "sigmoid":
            acc = _sigmoid(acc)
        o_ref[rs, :] = acc.astype(o_ref.dtype)


def matmul(a, w, layer, n, out_dtype, tm, tn, row_scale=None, col_scale=None, act=None, side=None, chunks=1,
           name="matmul"):
    m, k = a.shape
    assert m % tm == 0 and n % tn == 0 and w.shape[1:] == (k, n)
    grid = (m // tm, n // tn)
    in_specs = [pl.BlockSpec((tm, k), lambda i, j: (i, 0))]
    args = [a]
    if row_scale is not None:
        in_specs.append(pl.BlockSpec((tm, LANES), lambda i, j: (i, 0)))
        args.append(row_scale)
    if col_scale is not None:
        in_specs.append(pl.BlockSpec((1, tn), lambda i, j: (0, j)))
        args.append(col_scale)
    in_specs.append(pl.BlockSpec((None, k, tn), lambda i, j: (layer, 0, j)))
    args.append(w)
    out_specs = [pl.BlockSpec((tm, tn), lambda i, j: (i, j))]
    out_shape = [jax.ShapeDtypeStruct((m, n), out_dtype)]
    if side is not None:
        side_in, side_out, side_shape = _side_cast_specs(side, grid)
        in_specs.append(side_in)
        args.append(side.src)
        out_specs.append(side_out)
        out_shape.append(side_shape)
    res = pl.pallas_call(
        functools.partial(_matmul_kernel, act=act, row_scaled=row_scale is not None,
                          col_scaled=col_scale is not None, side=side is not None, chunks=chunks),
        grid=grid,
        in_specs=in_specs,
        out_specs=out_specs,
        out_shape=out_shape,
        compiler_params=_params("arbitrary", "arbitrary"),
        name=name,
    )(*args)
    return res if side is not None else res[0]


def _proj_norm_kernel(*refs, n_i, n_j, d, emit_next):
    if emit_next:
        a_ref, w_ref, x_ref, gp_ref, gn_ref, xo_ref, h_ref, r_ref, y_scr, acc_scr, done_scr, acc2_scr = refs
    else:
        a_ref, w_ref, x_ref, gp_ref, xo_ref, y_scr, acc_scr, done_scr = refs
    i = pl.program_id(0)
    j = pl.program_id(1)

    @pl.when((i == 0) & (j == 0))
    def _():
        y_scr[...] = jnp.zeros_like(y_scr)
        acc_scr[...] = jnp.zeros_like(acc_scr)
        done_scr[...] = jnp.zeros_like(done_scr)
        if emit_next:
            acc2_scr[...] = jnp.zeros_like(acc2_scr)

    first = j == 0
    y_prev = y_scr[j]
    y_new = _dot(a_ref[...], w_ref[...])
    y_scr[j] = y_new
    part = jnp.sum(y_new * y_new, axis=-1, keepdims=True)
    acc_old = acc_scr[...]
    done = jnp.where(first, acc_old, done_scr[...])
    done_scr[...] = done
    acc_scr[...] = jnp.where(first, part, acc_old + part)

    x_new = x_ref[...] + y_prev * lax.rsqrt(done * (1.0 / d) + RMS_EPS) * gp_ref[...]
    xo_ref[...] = x_new
    if emit_next:
        h_ref[...] = (x_new * gn_ref[...]).astype(h_ref.dtype)
        part2 = jnp.sum(x_new * x_new, axis=-1, keepdims=True)
        acc2 = jnp.where(first, part2, acc2_scr[...] + part2)
        acc2_scr[...] = acc2
        r_ref[...] = jnp.broadcast_to(lax.rsqrt(acc2 * (1.0 / d) + RMS_EPS), r_ref.shape)


def proj_norm(a, w, layer, x, g_post, g_next, tm, tn):
    m, k = a.shape
    d = w.shape[2]
    n_i, n_j = m // tm, d // tn
    emit_next = g_next is not None
    prev = lambda i: jnp.maximum(i - 1, 0)
    col = pl.BlockSpec((tm, tn), lambda i, j: (prev(i), j))
    col_out = pl.BlockSpec((tm, tn), lambda i, j: (prev(i), jnp.where(i == 0, 0, j)))
    vec = pl.BlockSpec((1, tn), lambda i, j: (0, j))
    in_specs = [pl.BlockSpec((tm, k), lambda i, j: (jnp.minimum(i, n_i - 1), 0)),
                pl.BlockSpec((None, k, tn), lambda i, j: (layer, 0, j)),
                col, vec]
    args = [a, w, x, g_post]
    out_specs = [col_out]
    out_shape = [jax.ShapeDtypeStruct((m, d), F32)]
    scratch = [pltpu.VMEM((n_j, tm, tn), F32), pltpu.VMEM((tm, 1), F32), pltpu.VMEM((tm, 1), F32)]
    if emit_next:
        in_specs.append(vec)
        args.append(g_next)
        out_specs += [col_out, pl.BlockSpec((tm, LANES), lambda i, j: (prev(i), 0))]
        out_shape += [jax.ShapeDtypeStruct((m, d), BF16), jax.ShapeDtypeStruct((m, LANES), F32)]
        scratch.append(pltpu.VMEM((tm, 1), F32))
    res = pl.pallas_call(
        functools.partial(_proj_norm_kernel, n_i=n_i, n_j=n_j, d=d, emit_next=emit_next),
        grid=(n_i + 1, n_j),
        in_specs=in_specs,
        out_specs=out_specs,
        out_shape=out_shape,
        scratch_shapes=scratch,
        compiler_params=_params("arbitrary", "arbitrary"),
        name="proj_norm" if emit_next else "proj_norm_last",
    )(*args)
    return res if emit_next else (res[0], None, None)


def _latent_kernel(h_ref, r_ref, w_ref, gq_ref, gkv_ref, cos_ref, sin_ref, src_ref,
                   cq_ref, ckv_ref, kr_ref, z_ref, dst_ref, *, chunks):
    dst_ref[...] = src_ref[...].astype(dst_ref.dtype)
    q_end = MLA_Q_LORA
    kv_end = q_end + MLA_KV_LORA
    for rs in _row_chunks(h_ref.shape[0], chunks):
        acc = _dot(h_ref[rs, :], w_ref[...]) * r_ref[rs, 0:1]
        cq = acc[:, :q_end]
        ckv = acc[:, q_end:kv_end]
        cq_ref[rs, :] = (cq * _inv_rms(cq) * gq_ref[...]).astype(cq_ref.dtype)
        ckv_ref[rs, :] = (ckv * _inv_rms(ckv) * gkv_ref[...]).astype(ckv_ref.dtype)
        slab = acc[:, kv_end:kv_end + LANES]
        rot = slab * cos_ref[rs, :] + pltpu.roll(slab, LANES // 2, 1) * sin_ref[rs, :]
        kr_ref[rs, :] = rot.astype(kr_ref.dtype)
        z_ref[rs, :] = acc[:, kv_end + LANES:kv_end + 2 * LANES]


def latent_proj(h, r, w_lat, layer, g_q, g_kv, cos_t, sin_t, side, tm=512, chunks=4):
    m, d = h.shape
    n = LAT_WIDTH
    grid = (m // tm,)
    row = lambda width: pl.BlockSpec((tm, width), lambda i: (i, 0))
    const = lambda c: pl.BlockSpec((1, c), lambda i: (0, 0))
    side_in, side_out, side_shape = _side_cast_specs(side, grid)
    return pl.pallas_call(
        functools.partial(_latent_kernel, chunks=chunks),
        grid=grid,
        in_specs=[row(d), row(LANES), pl.BlockSpec((None, d, n), lambda i: (layer, 0, 0)),
                  const(MLA_Q_LORA), const(MLA_KV_LORA), row(LANES), row(LANES), side_in],
        out_specs=[row(MLA_Q_LORA), row(MLA_KV_LORA), row(LANES), row(LANES), side_out],
        out_shape=[jax.ShapeDtypeStruct((m, MLA_Q_LORA), BF16), jax.ShapeDtypeStruct((m, MLA_KV_LORA), BF16),
                   jax.ShapeDtypeStruct((m, LANES), BF16), jax.ShapeDtypeStruct((m, LANES), F32), side_shape],
        compiler_params=_params("arbitrary"),
        name="latent_proj",
    )(h, r, w_lat, g_q, g_kv, cos_t, sin_t, side.src)


def _q_up_kernel(a_ref, w_ref, cos_ref, sin_ref, o_ref, *, scale):
    acc = _dot(a_ref[...], w_ref[...])
    c = cos_ref[...]
    s = sin_ref[...]
    for h in range(MLA_HEADS):
        base = 2 * LANES * h
        nope = acc[:, base:base + LANES]
        slab = acc[:, base + LANES:base + 2 * LANES]
        rot = slab * c + pltpu.roll(slab, LANES // 2, 1) * s
        o_ref[:, base:base + LANES] = (nope * scale).astype(o_ref.dtype)
        o_ref[:, base + LANES:base + 2 * LANES] = (rot * scale).astype(o_ref.dtype)


def q_up_proj(cqn, w_uq, layer, cos_t, sin_t, tm=1024):
    m, k = cqn.shape
    n = w_uq.shape[2]
    return pl.pallas_call(
        functools.partial(_q_up_kernel, scale=MLA_QK ** -0.5 * LOG2_E),
        grid=(m // tm,),
        in_specs=[pl.BlockSpec((tm, k), lambda i: (i, 0)), pl.BlockSpec((None, k, n), lambda i: (layer, 0, 0)),
                  pl.BlockSpec((tm, LANES), lambda i: (i, 0)), pl.BlockSpec((tm, LANES), lambda i: (i, 0))],
        out_specs=pl.BlockSpec((tm, n), lambda i: (i, 0)),
        out_shape=jax.ShapeDtypeStruct((m, n), BF16),
        compiler_params=_params("parallel"),
        name="q_up_proj",
    )(cqn, w_uq, cos_t, sin_t)


class _Head(NamedTuple):
    q: Any
    q_cols: slice
    k: Any
    k_cols: slice
    v: Any
    v_cols: slice
    o: Any
    o_cols: slice
    fq: Any = None
    fk_row: Any = None


def _causal_blocks(heads, seq, tq):
    row = lax.broadcasted_iota(jnp.int32, (tq, tq), 0)
    col = lax.broadcasted_iota(jnp.int32, (tq, tq), 1)
    tri = col <= row
    for i in range(seq // tq):
        lo, hi = i * tq, (i + 1) * tq
        for hd in heads:
            qi = hd.q[lo:hi, hd.q_cols]
            s_d = _dot_nt(qi, hd.k[lo:hi, hd.k_cols])
            if hd.fk_row is not None:
                s_d = s_d - hd.fk_row[:, lo:hi]
            s_d = jnp.where(tri, s_d, -jnp.inf)
            mx = jnp.max(s_d, axis=-1, keepdims=True)
            if i > 0:
                s_o = _dot_nt(qi, hd.k[:lo, hd.k_cols])
                if hd.fk_row is not None:
                    s_o = s_o - hd.fk_row[:, :lo]
                mx = jnp.maximum(mx, jnp.max(s_o, axis=-1, keepdims=True))
            if hd.fq is not None:
                fq = hd.fq[lo:hi, 0:1]
                off = fq - (mx + fq)
            else:
                off = -mx
            p_d = jnp.exp2(s_d + off)
            den = jnp.sum(p_d, axis=-1, keepdims=True)
            acc = _dot(p_d.astype(BF16), hd.v[lo:hi, hd.v_cols])
            if i > 0:
                p_o = jnp.exp2(s_o + off)
                den = den + jnp.sum(p_o, axis=-1, keepdims=True)
                acc = acc + _dot(p_o.astype(BF16), hd.v[:lo, hd.v_cols])
            hd.o[lo:hi, hd.o_cols] = (acc / den).astype(hd.o.dtype)


def _mla_attn_kernel(q_ref, kv_ref, kr_ref, o_ref, k_scr, *, seq, tq, hps):
    heads = []
    for u in range(hps):
        base = 2 * LANES * u
        k_scr[u, :, :LANES] = kv_ref[:, base:base + LANES]
        k_scr[u, :, LANES:] = kr_ref[...]
        heads.append(_Head(q_ref, slice(base, base + 2 * LANES), k_scr.at[u], slice(None),
                           kv_ref, slice(base + LANES, base + 2 * LANES), o_ref, slice(LANES * u, LANES * (u + 1))))
    _causal_blocks(heads, seq, tq)


def mla_attention(q, kv, k_rot, batch, seq, hps=HEADS_PER_STEP):
    m = q.shape[0]
    return pl.pallas_call(
        functools.partial(_mla_attn_kernel, seq=seq, tq=Q_TILE, hps=hps),
        grid=(batch, MLA_HEADS // hps),
        in_specs=[pl.BlockSpec((seq, 2 * LANES * hps), lambda b, g: (b, g)),
                  pl.BlockSpec((seq, 2 * LANES * hps), lambda b, g: (b, g)),
                  pl.BlockSpec((seq, LANES), lambda b, g: (b, 0))],
        out_specs=pl.BlockSpec((seq, LANES * hps), lambda b, g: (b, g)),
        out_shape=jax.ShapeDtypeStruct((m, MLA_HEADS * LANES), BF16),
        scratch_shapes=[pltpu.VMEM((hps, seq, 2 * LANES), BF16)],
        compiler_params=_params("parallel", "arbitrary"),
        name="mla_attention",
    )(q, kv, k_rot)


def _fox_attn_kernel(q_ref, k_ref, v_ref, ft_ref, o_ref, fq_scr, *, seq, tq, hps):
    g = pl.program_id(1)
    heads = []
    for u in range(hps):
        fk_row = ft_ref[0, pl.ds(g * hps + u, 1), :]
        fq_scr[u] = jnp.transpose(jnp.broadcast_to(fk_row, (LANES, seq)))
        cols = slice(LANES * u, LANES * (u + 1))
        heads.append(_Head(q_ref, cols, k_ref, cols, v_ref, cols, o_ref, cols, fq_scr.at[u], fk_row))
    _causal_blocks(heads, seq, tq)


def fox_attention(qkv, f_t, batch, seq, hps=HEADS_PER_STEP):
    m = qkv.shape[0]
    nh = FOX_HEADS
    ng = nh // hps
    blk = lambda part: pl.BlockSpec((seq, LANES * hps), lambda b, g: (b, part * ng + g))
    return pl.pallas_call(
        functools.partial(_fox_attn_kernel, seq=seq, tq=Q_TILE, hps=hps),
        grid=(batch, ng),
        in_specs=[blk(0), blk(1), blk(2), pl.BlockSpec((1, nh, seq), lambda b, g: (b, 0, 0))],
        out_specs=pl.BlockSpec((seq, LANES * hps), lambda b, g: (b, g)),
        out_shape=jax.ShapeDtypeStruct((m, nh * LANES), BF16),
        scratch_shapes=[pltpu.VMEM((hps, seq, LANES), F32)],
        compiler_params=_params("parallel", "arbitrary"),
        name="fox_attention",
    )(qkv, qkv, qkv, f_t)


def _fox_decay_kernel(z_ref, b_ref, ft_ref, *, seq):
    x = z_ref[...] + b_ref[...]
    log_f = jnp.minimum(x, 0.0) - jnp.log1p(jnp.exp(-jnp.abs(x)))
    t = jnp.transpose(log_f)[0:FOX_HEADS, :]
    lane = lax.broadcasted_iota(jnp.int32, t.shape, 1)
    shift = 1
    while shift < seq:
        t = t + jnp.where(lane >= shift, pltpu.roll(t, shift, 1), 0.0)
        shift *= 2
    ft_ref[0] = t * LOG2_E


def fox_decay(z_slab, b_row, batch, seq):
    return pl.pallas_call(
        functools.partial(_fox_decay_kernel, seq=seq),
        grid=(batch,),
        in_specs=[pl.BlockSpec((seq, LANES), lambda b: (b, 0)), pl.BlockSpec((1, LANES), lambda b: (0, 0))],
        out_specs=pl.BlockSpec((1, FOX_HEADS, seq), lambda b: (b, 0, 0)),
        out_shape=jax.ShapeDtypeStruct((batch, FOX_HEADS, seq), F32),
        compiler_params=_params("parallel"),
        name="fox_decay",
    )(z_slab, b_row)


def _swa_kernel(slope_ref, sink_ref, q_ref, kp_ref, kc_ref, vp_ref, vc_ref, qpos_ref, pp_ref, pc_ref, o_ref):
    n = pl.program_id(1)
    w = WINDOW
    i = lax.broadcasted_iota(jnp.int32, (w, w), 0)
    j = lax.broadcasted_iota(jnp.int32, (w, w), 1)
    valid = jnp.concatenate([(j > i) & (n > 0), j <= i], axis=1)
    qpos = qpos_ref[:, 0:1]
    dist = jnp.concatenate([qpos - pp_ref[0], qpos - pc_ref[0]], axis=1).astype(F32)
    neg_dist = jnp.where(valid, -dist, -jnp.inf)
    for kh in range(SWA_KV_HEADS):
        ks = slice(kh * HEAD_DIM, (kh + 1) * HEAD_DIM)
        k_win = jnp.concatenate([kp_ref[:, ks], kc_ref[:, ks]], axis=0)
        v_win = jnp.concatenate([vp_ref[:, ks], vc_ref[:, ks]], axis=0)
        for g in range(SWA_GROUP):
            hd = kh * SWA_GROUP + g
            hs = slice(hd * HEAD_DIM, (hd + 1) * HEAD_DIM)
            s = _dot_nt(q_ref[:, hs], k_win) + slope_ref[hd] * neg_dist
            sink = sink_ref[hd]
            mx = jnp.maximum(jnp.max(s, axis=-1, keepdims=True), sink)
            e = jnp.exp2(s - mx)
            den = jnp.sum(e, axis=-1, keepdims=True) + jnp.exp2(sink - mx)
            o_ref[:, hs] = (_dot(e.astype(BF16), v_win) / den).astype(o_ref.dtype)


def swa_attention(qkv, pos_b, pos_row, slopes, sinks, batch, seq):
    m = qkv.shape[0]
    w = WINDOW
    nb = seq // w
    qw = SWA_HEADS * HEAD_DIM
    kvw = SWA_KV_HEADS * HEAD_DIM
    cur = lambda b, n: b * nb + n
    prev = lambda b, n: b * nb + jnp.maximum(n - 1, 0)
    smem = pl.BlockSpec(memory_space=pltpu.SMEM)
    return pl.pallas_call(
        _swa_kernel,
        grid=(batch, nb),
        in_specs=[smem, smem,
                  pl.BlockSpec((w, qw), lambda b, n: (cur(b, n), 0)),
                  pl.BlockSpec((w, kvw), lambda b, n: (prev(b, n), qw // kvw)),
                  pl.BlockSpec((w, kvw), lambda b, n: (cur(b, n), qw // kvw)),
                  pl.BlockSpec((w, kvw), lambda b, n: (prev(b, n), qw // kvw + 1)),
                  pl.BlockSpec((w, kvw), lambda b, n: (cur(b, n), qw // kvw + 1)),
                  pl.BlockSpec((w, LANES), lambda b, n: (cur(b, n), 0)),
                  pl.BlockSpec((1, 1, w), lambda b, n: (prev(b, n), 0, 0)),
                  pl.BlockSpec((1, 1, w), lambda b, n: (cur(b, n), 0, 0))],
        out_specs=pl.BlockSpec((w, qw), lambda b, n: (cur(b, n), 0)),
        out_shape=jax.ShapeDtypeStruct((m, qw), BF16),
        compiler_params=_params("parallel", "arbitrary"),
        name="swa_attention",
    )(slopes, sinks, qkv, qkv, qkv, qkv, qkv, pos_b, pos_row, pos_row)


def _merge_kernel(mla_ref, swa_lo_ref, swa_hi_ref, fox_ref, w0_ref, w1_ref, w2_ref, w3_ref,
                  g0_ref, g1_ref, g2_ref, o_ref, *, chunks):
    for rs in _row_chunks(o_ref.shape[0], chunks):
        acc = g0_ref[rs, :].astype(F32) * _dot(mla_ref[rs, :], w0_ref[...])
        swa = _dot(swa_lo_ref[rs, :], w1_ref[...]) + _dot(swa_hi_ref[rs, :], w2_ref[...])
        acc = acc + g1_ref[rs, :].astype(F32) * swa
        acc = acc + g2_ref[rs, :].astype(F32) * _dot(fox_ref[rs, :], w3_ref[...])
        o_ref[rs, :] = acc.astype(o_ref.dtype)


def gated_merge(o_mla, o_swa, o_fox, w_branch, layer, gates, tm=1024, tn=512, chunks=1):
    m, kb = o_mla.shape
    d = w_branch.shape[2]
    nj = d // tn
    assert o_swa.shape[1] == 2 * kb and o_fox.shape[1] == kb and w_branch.shape[1] == 4 * kb
    a_spec = lambda cb: pl.BlockSpec((tm, kb), lambda i, j: (i, cb))
    w_spec = lambda rb: pl.BlockSpec((None, kb, tn), lambda i, j: (layer, rb, j))
    g_spec = lambda br: pl.BlockSpec((tm, tn), lambda i, j: (i, br * nj + j))
    return pl.pallas_call(
        functools.partial(_merge_kernel, chunks=chunks),
        grid=(m // tm, nj),
        in_specs=[a_spec(0), a_spec(0), a_spec(1), a_spec(0)] + [w_spec(rb) for rb in range(4)]
                 + [g_spec(br) for br in range(3)],
        out_specs=pl.BlockSpec((tm, tn), lambda i, j: (i, j)),
        out_shape=jax.ShapeDtypeStruct((m, d), BF16),
        compiler_params=_params("parallel", "arbitrary"),
        name="gated_merge",
    )(o_mla, o_swa, o_swa, o_fox, w_branch, w_branch, w_branch, w_branch, gates, gates, gates)


def _ffn_up_kernel(h_ref, r_ref, wg_ref, wu_ref, src_ref, o_ref, dst_ref, *, chunks):
    dst_ref[...] = src_ref[...].astype(dst_ref.dtype)
    for rs in _row_chunks(h_ref.shape[0], chunks):
        r = r_ref[rs, 0:1]
        gate = _dot(h_ref[rs, :], wg_ref[...]) * r
        up = _dot(h_ref[rs, :], wu_ref[...]) * r
        o_ref[rs, :] = (gate * _sigmoid(gate) * up).astype(o_ref.dtype)


def ffn_up(h, r, w_gate_up, layer, side, tm=2048, tn=256, chunks=8):
    m, d = h.shape
    n = w_gate_up.shape[2] // 2
    assert n % tn == 0
    nj = n // tn
    grid = (m // tm, nj)
    side_in, side_out, side_shape = _side_cast_specs(side, grid)
    return pl.pallas_call(
        functools.partial(_ffn_up_kernel, chunks=chunks),
        grid=grid,
        in_specs=[pl.BlockSpec((tm, d), lambda i, j: (i, 0)),
                  pl.BlockSpec((tm, LANES), lambda i, j: (i, 0)),
                  pl.BlockSpec((None, d, tn), lambda i, j: (layer, 0, j)),
                  pl.BlockSpec((None, d, tn), lambda i, j: (layer, 0, nj + j)),
                  side_in],
        out_specs=[pl.BlockSpec((tm, tn), lambda i, j: (i, j)), side_out],
        out_shape=[jax.ShapeDtypeStruct((m, n), BF16), side_shape],
        compiler_params=_params("arbitrary", "arbitrary"),
        name="ffn_up",
    )(h, r, w_gate_up, w_gate_up, side.src)


def _regroup_kernel(main_ref, next_ref, o_ref, *, shift):
    x = jnp.concatenate([main_ref[shift:, :], next_ref[:shift, :]], axis=0)
    o_ref[...] = jnp.transpose(x).astype(o_ref.dtype)


def regroup_cast(w_t, row0, width, tile=512, tk=2048):
    depth, _, k = w_t.shape
    shift = row0 % LANES
    base = row0 - shift
    assert shift % 8 == 0 and shift and base % tile == 0 and width % tile == 0 and k % tk == 0
    return pl.pallas_call(
        functools.partial(_regroup_kernel, shift=shift),
        grid=(depth, k // tk, width // tile),
        in_specs=[pl.BlockSpec((None, tile, tk), lambda l, kb, c: (l, base // tile + c, kb)),
                  pl.BlockSpec((None, LANES, tk), lambda l, kb, c: (l, (base + (c + 1) * tile) // LANES, kb))],
        out_specs=pl.BlockSpec((None, tk, tile), lambda l, kb, c: (l, kb, c)),
        out_shape=jax.ShapeDtypeStruct((depth, k, width), BF16),
        compiler_params=_params("parallel", "parallel", "arbitrary"),
        name="regroup_cast",
    )(w_t, w_t)


def _latent_weight_kernel(main_ref, kr_ref, z_ref, o_ref):
    half = MLA_ROPE // 2
    kr = kr_ref[...]
    pad = jnp.zeros((LANES - FOX_HEADS, kr.shape[1]), kr.dtype)
    tail = jnp.concatenate([kr, kr[half:, :], kr[:half, :], z_ref[...], pad], axis=0)
    n_main = main_ref.shape[0]
    o_ref[:, :n_main] = jnp.transpose(main_ref[...]).astype(o_ref.dtype)
    o_ref[:, n_main:] = jnp.transpose(tail).astype(o_ref.dtype)


def latent_weights(w_t, kr_row, z_row, tk=1024):
    depth, _, k = w_t.shape
    n_main = kr_row
    assert n_main % LANES == 0 and kr_row % MLA_ROPE == 0 and z_row % FOX_HEADS == 0 and FOX_HEADS % 8 == 0
    return pl.pallas_call(
        _latent_weight_kernel,
        grid=(depth, k // tk),
        in_specs=[pl.BlockSpec((None, n_main, tk), lambda l, kb: (l, 0, kb)),
                  pl.BlockSpec((None, MLA_ROPE, tk), lambda l, kb: (l, kr_row // MLA_ROPE, kb)),
                  pl.BlockSpec((None, FOX_HEADS, tk), lambda l, kb: (l, z_row // FOX_HEADS, kb))],
        out_specs=pl.BlockSpec((None, tk, LAT_WIDTH), lambda l, kb: (l, kb, 0)),
        out_shape=jax.ShapeDtypeStruct((depth, k, LAT_WIDTH), BF16),
        compiler_params=_params("parallel", "parallel"),
        name="latent_weights",
    )(w_t, w_t, w_t)


def _input_proj_weights(w_in):
    d = w_in.shape[1]
    w_t = jnp.swapaxes(w_in, 1, 2)
    o = 0
    cuts = {}
    for name, width in (("cq", MLA_Q_LORA), ("ckv", MLA_KV_LORA), ("kr", MLA_ROPE), ("swa", SWA_WIDTH),
                        ("fox", FOX_WIDTH), ("z", FOX_HEADS), ("gate", 3 * d)):
        cuts[name] = (o, o + width)
        o += width
    assert cuts["cq"][0] == 0 and cuts["kr"][0] == MLA_Q_LORA + MLA_KV_LORA
    w_lat = latent_weights(w_t, cuts["kr"][0], cuts["z"][0])
    w_gate = regroup_cast(w_t, cuts["gate"][0], 3 * d)
    w_fox = regroup_cast(w_t, cuts["fox"][0], FOX_WIDTH)
    w_swa = regroup_cast(w_t, cuts["swa"][0], SWA_WIDTH)
    return w_gate, w_fox, w_swa, w_lat


def _q_up_weights(w_uq):
    depth = w_uq.shape[0]
    half = MLA_ROPE // 2
    uq = w_uq.reshape(depth, MLA_Q_LORA, MLA_HEADS, MLA_QK)
    rope = uq[..., MLA_NOPE:]
    rope_swapped = jnp.concatenate([rope[..., half:], rope[..., :half]], axis=-1)
    w_q = jnp.concatenate([uq[..., :MLA_NOPE], rope, rope_swapped], axis=-1)
    return w_q.reshape(depth, MLA_Q_LORA, MLA_HEADS * 2 * LANES).astype(BF16)


def _rope_consts():
    half = MLA_ROPE // 2
    inv_freq = ROPE_THETA ** (-jnp.arange(half, dtype=F32) / half)
    zeros = jnp.zeros((LANES - MLA_ROPE,), F32)
    ones = jnp.ones((half,), F32)
    rows = jnp.stack([jnp.concatenate([inv_freq, inv_freq, zeros]),
                      jnp.concatenate([ones, ones, zeros]),
                      jnp.concatenate([-ones, ones, zeros])])
    return jnp.concatenate([rows, jnp.zeros((5, LANES), F32)], axis=0)


def kernel(x, positions, g_mix_pre, g_mix_post, g_ffn_pre, g_ffn_post, w_in, g_q_lora, w_uq, g_kv_lora, w_ukv,
           b_forget, swa_sinks, w_branch, w_out, w_gate_up, w_down):
    batch, seq, d = x.shape
    depth = w_in.shape[0]
    m = batch * seq
    xs = x.reshape(m, d)
    pos_b = jnp.broadcast_to(positions.reshape(m, 1), (m, LANES))
    pos_row = positions.reshape(m // WINDOW, 1, WINDOW)
    slopes = 2.0 ** (-8.0 * jnp.arange(1, SWA_HEADS + 1, dtype=F32) / SWA_HEADS)
    cos_t, sin_t = rope_tables(pos_b, _rope_consts())
    vec = lambda g: g.reshape(1, -1)

    w_gate, w_fox, w_swa, w_lat = _input_proj_weights(w_in)
    w_q = _q_up_weights(w_uq)
    w_kv = w_ukv.astype(BF16)
    b_rows = jnp.zeros((depth, 1, LANES), F32).at[:, 0, :FOX_HEADS].set(b_forget)
    q_scale = HEAD_DIM ** -0.5 * LOG2_E
    qkv_scale = lambda q_width, width: jnp.concatenate(
        [jnp.full((1, q_width), q_scale, F32), jnp.ones((1, width - q_width), F32)], axis=1)
    fox_scale = qkv_scale(FOX_HEADS * HEAD_DIM, FOX_WIDTH)
    swa_scale = qkv_scale(SWA_HEADS * HEAD_DIM, SWA_WIDTH)

    h, r = prenorm(xs, vec(g_mix_pre[0]))
    for l in range(depth):
        gates, w_gu = matmul(h, w_gate, l, 3 * d, BF16, 1024, 1024, row_scale=r, act="sigmoid", chunks=1,
                             side=SideCast(w_gate_up, l), name="gate_proj")
        qkv_fox, w_br = matmul(h, w_fox, l, FOX_WIDTH, BF16, 1024, 1024, row_scale=r, col_scale=fox_scale,
                               chunks=1, side=SideCast(w_branch, l), name="fox_proj")
        qkv_swa = matmul(h, w_swa, l, SWA_WIDTH, BF16, 1024, SWA_WIDTH // 2, row_scale=r, col_scale=swa_scale,
                         chunks=1, name="swa_proj")
        cqn, ckvn, k_rot, z_slab, w_o = latent_proj(h, r, w_lat, l, vec(g_q_lora[l]), vec(g_kv_lora[l]),
                                                    cos_t, sin_t, SideCast(w_out, l))
        q_mla = q_up_proj(cqn, w_q, l, cos_t, sin_t)
        kv_mla = matmul(ckvn, w_kv, l, w_kv.shape[2], BF16, 1024, 1024, name="kv_up_proj")
        f_t = fox_decay(z_slab, b_rows[l], batch, seq)
        o_mla = mla_attention(q_mla, kv_mla, k_rot, batch, seq)
        o_swa = swa_attention(qkv_swa, pos_b, pos_row, slopes * LOG2_E, swa_sinks[l].astype(F32) * LOG2_E,
                              batch, seq)
        o_fox = fox_attention(qkv_fox, f_t, batch, seq)
        merged = gated_merge(o_mla, o_swa, o_fox, w_br, 0, gates)
        xs, h2, r2 = proj_norm(merged, w_o, 0, xs, vec(g_mix_post[l]), vec(g_ffn_pre[l]), 512, 1024)
        act, w_dn = ffn_up(h2, r2, w_gu, 0, SideCast(w_down, l))
        g_next = vec(g_mix_pre[l + 1]) if l + 1 < depth else None
        xs, h, r = proj_norm(act, w_dn, 0, xs, vec(g_ffn_post[l]), g_next, 512, 256)
    return xs.reshape(batch, seq, d)
```

```python
import functools
import math
from typing import Any, NamedTuple

import jax
import jax.numpy as jnp
from jax import lax
from jax.experimental import pallas as pl
from jax.experimental.pallas import tpu as pltpu

F32 = jnp.float32
BF16 = jnp.bfloat16

VMEM_LIMIT_BYTES = 56 * 1024 * 1024
LANES = 128

HEAD_DIM = 128
MLA_HEADS = 8
MLA_Q_LORA = 1024
MLA_KV_LORA = 512
MLA_NOPE = 128
MLA_ROPE = 64
MLA_QK = MLA_NOPE + MLA_ROPE
SWA_HEADS = 16
SWA_KV_HEADS = 2
SWA_GROUP = SWA_HEADS // SWA_KV_HEADS
WINDOW = 128
FOX_HEADS = 8
ROPE_THETA = 10000.0
RMS_EPS = 1e-6
LOG2_E = math.log2(math.e)
Q_TILE = 512
HEADS_PER_STEP = 2

SWA_WIDTH = (SWA_HEADS + 2 * SWA_KV_HEADS) * HEAD_DIM
FOX_WIDTH = 3 * FOX_HEADS * HEAD_DIM
LAT_WIDTH = MLA_Q_LORA + MLA_KV_LORA + 2 * LANES


def _params(*sem):
    return pltpu.CompilerParams(dimension_semantics=sem, vmem_limit_bytes=VMEM_LIMIT_BYTES)


def _inv_rms(x):
    return lax.rsqrt(jnp.mean(x * x, axis=-1, keepdims=True) + RMS_EPS)


def _dot(a, b):
    return jnp.dot(a, b, preferred_element_type=F32)


def _dot_nt(a, b):
    return lax.dot_general(a, b, (((1,), (1,)), ((), ())), preferred_element_type=F32)


def _sigmoid(x):
    return 0.5 * jnp.tanh(0.5 * x) + 0.5


def _row_chunks(rows, chunks):
    step = rows // chunks
    return [slice(c * step, (c + 1) * step) for c in range(chunks)]


def _rope_table_kernel(pos_ref, c_ref, cos_ref, sin_ref):
    ang = pos_ref[...].astype(F32) * c_ref[0:1, :]
    cos_ref[...] = jnp.cos(ang) * c_ref[1:2, :]
    sin_ref[...] = jnp.sin(ang) * c_ref[2:3, :]


def rope_tables(pos_b, consts, tm=2048):
    m = pos_b.shape[0]
    spec = pl.BlockSpec((tm, LANES), lambda i: (i, 0))
    return pl.pallas_call(
        _rope_table_kernel,
        grid=(m // tm,),
        in_specs=[spec, pl.BlockSpec((8, LANES), lambda i: (0, 0))],
        out_specs=[spec, spec],
        out_shape=[jax.ShapeDtypeStruct((m, LANES), F32)] * 2,
        compiler_params=_params("parallel"),
        name="rope_tables",
    )(pos_b, consts)


def _prenorm_kernel(x_ref, g_ref, h_ref, r_ref):
    x = x_ref[...]
    h_ref[...] = (x * g_ref[...]).astype(h_ref.dtype)
    r_ref[...] = jnp.broadcast_to(_inv_rms(x), r_ref.shape)


def prenorm(x, gain, tm=256):
    m, d = x.shape
    return pl.pallas_call(
        _prenorm_kernel,
        grid=(m // tm,),
        in_specs=[pl.BlockSpec((tm, d), lambda i: (i, 0)), pl.BlockSpec((1, d), lambda i: (0, 0))],
        out_specs=[pl.BlockSpec((tm, d), lambda i: (i, 0)), pl.BlockSpec((tm, LANES), lambda i: (i, 0))],
        out_shape=[jax.ShapeDtypeStruct((m, d), BF16), jax.ShapeDtypeStruct((m, LANES), F32)],
        compiler_params=_params("parallel"),
        name="prenorm",
    )(x, gain)


class SideCast(NamedTuple):
    src: Any
    layer: int


BF16_SUBLANES = 16


def _side_cast_specs(side, grid):
    _, rows, cols = side.src.shape
    steps = math.prod(grid)
    n_blocks = max(nb for nb in range(1, steps + 1) if rows % (nb * BF16_SUBLANES) == 0)
    br = rows // n_blocks

    def row_block(*ids):
        step = 0
        for extent, idx in zip(grid, ids):
            step = step * extent + idx
        return jnp.minimum(step, n_blocks - 1)

    in_spec = pl.BlockSpec((None, br, cols), lambda *ids: (side.layer, row_block(*ids), 0))
    out_spec = pl.BlockSpec((None, br, cols), lambda *ids: (0, row_block(*ids), 0))
    return in_spec, out_spec, jax.ShapeDtypeStruct((1, rows, cols), BF16)


def _matmul_kernel(*refs, act, row_scaled, col_scaled, side, chunks):
    refs = list(refs)
    a_ref = refs.pop(0)
    r_ref = refs.pop(0) if row_scaled else None
    c_ref = refs.pop(0) if col_scaled else None
    w_ref = refs.pop(0)
    if side:
        src_ref, o_ref, dst_ref = refs
        dst_ref[...] = src_ref[...].astype(dst_ref.dtype)
    else:
        (o_ref,) = refs
    for rs in _row_chunks(a_ref.shape[0], chunks):
        acc = _dot(a_ref[rs, :], w_ref[...])
        if row_scaled:
            acc = acc * r_ref[rs, 0:1]
        if col_scaled:
            acc = acc * c_ref[...]
        if act == "sigmoid":
            acc = _sigmoid(acc)
        o_ref[rs, :] = acc.astype(o_ref.dtype)


def matmul(a, w, layer, n, out_dtype, tm, tn, row_scale=None, col_scale=None, act=None, side=None, chunks=1,
           name="matmul"):
    m, k = a.shape
    assert m % tm == 0 and n % tn == 0 and w.shape[1:] == (k, n)
    grid = (m // tm, n // tn)
    in_specs = [pl.BlockSpec((tm, k), lambda i, j: (i, 0))]
    args = [a]
    if row_scale is not None:
        in_specs.append(pl.BlockSpec((tm, LANES), lambda i, j: (i, 0)))
        args.append(row_scale)
    if col_scale is not None:
        in_specs.append(pl.BlockSpec((1, tn), lambda i, j: (0, j)))
        args.append(col_scale)
    in_specs.append(pl.BlockSpec((None, k, tn), lambda i, j: (layer, 0, j)))
    args.append(w)
    out_specs = [pl.BlockSpec((tm, tn), lambda i, j: (i, j))]
    out_shape = [jax.ShapeDtypeStruct((m, n), out_dtype)]
    if side is not None:
        side_in, side_out, side_shape = _side_cast_specs(side, grid)
        in_specs.append(side_in)
        args.append(side.src)
        out_specs.append(side_out)
        out_shape.append(side_shape)
    res = pl.pallas_call(
        functools.partial(_matmul_kernel, act=act, row_scaled=row_scale is not None,
                          col_scaled=col_scale is not None, side=side is not None, chunks=chunks),
        grid=grid,
        in_specs=in_specs,
        out_specs=out_specs,
        out_shape=out_shape,
        compiler_params=_params("arbitrary", "arbitrary"),
        name=name,
    )(*args)
    return res if side is not None else res[0]


def _proj_norm_kernel(*refs, n_i, n_j, d, emit_next):
    if emit_next:
        a_ref, w_ref, x_ref, gp_ref, gn_ref, xo_ref, h_ref, r_ref, y_scr, acc_scr, done_scr, acc2_scr = refs
    else:
        a_ref, w_ref, x_ref, gp_ref, xo_ref, y_scr, acc_scr, done_scr = refs
    i = pl.program_id(0)
    j = pl.program_id(1)

    @pl.when((i == 0) & (j == 0))
    def _():
        y_scr[...] = jnp.zeros_like(y_scr)
        acc_scr[...] = jnp.zeros_like(acc_scr)
        done_scr[...] = jnp.zeros_like(done_scr)
        if emit_next:
            acc2_scr[...] = jnp.zeros_like(acc2_scr)

    first = j == 0
    y_prev = y_scr[j]
    y_new = _dot(a_ref[...], w_ref[...])
    y_scr[j] = y_new
    part = jnp.sum(y_new * y_new, axis=-1, keepdims=True)
    acc_old = acc_scr[...]
    done = jnp.where(first, acc_old, done_scr[...])
    done_scr[...] = done
    acc_scr[...] = jnp.where(first, part, acc_old + part)

    x_new = x_ref[...] + y_prev * lax.rsqrt(done * (1.0 / d) + RMS_EPS) * gp_ref[...]
    xo_ref[...] = x_new
    if emit_next:
        h_ref[...] = (x_new * gn_ref[...]).astype(h_ref.dtype)
        part2 = jnp.sum(x_new * x_new, axis=-1, keepdims=True)
        acc2 = jnp.where(first, part2, acc2_scr[...] + part2)
        acc2_scr[...] = acc2
        r_ref[...] = jnp.broadcast_to(lax.rsqrt(acc2 * (1.0 / d) + RMS_EPS), r_ref.shape)


def proj_norm(a, w, layer, x, g_post, g_next, tm, tn):
    m, k = a.shape
    d = w.shape[2]
    n_i, n_j = m // tm, d // tn
    emit_next = g_next is not None
    prev = lambda i: jnp.maximum(i - 1, 0)
    col = pl.BlockSpec((tm, tn), lambda i, j: (prev(i), j))
    col_out = pl.BlockSpec((tm, tn), lambda i, j: (prev(i), jnp.where(i == 0, 0, j)))
    vec = pl.BlockSpec((1, tn), lambda i, j: (0, j))
    in_specs = [pl.BlockSpec((tm, k), lambda i, j: (jnp.minimum(i, n_i - 1), 0)),
                pl.BlockSpec((None, k, tn), lambda i, j: (layer, 0, j)),
                col, vec]
    args = [a, w, x, g_post]
    out_specs = [col_out]
    out_shape = [jax.ShapeDtypeStruct((m, d), F32)]
    scratch = [pltpu.VMEM((n_j, tm, tn), F32), pltpu.VMEM((tm, 1), F32), pltpu.VMEM((tm, 1), F32)]
    if emit_next:
        in_specs.append(vec)
        args.append(g_next)
        out_specs += [col_out, pl.BlockSpec((tm, LANES), lambda i, j: (prev(i), 0))]
        out_shape += [jax.ShapeDtypeStruct((m, d), BF16), jax.ShapeDtypeStruct((m, LANES), F32)]
        scratch.append(pltpu.VMEM((tm, 1), F32))
    res = pl.pallas_call(
        functools.partial(_proj_norm_kernel, n_i=n_i, n_j=n_j, d=d, emit_next=emit_next),
        grid=(n_i + 1, n_j),
        in_specs=in_specs,
        out_specs=out_specs,
        out_shape=out_shape,
        scratch_shapes=scratch,
        compiler_params=_params("arbitrary", "arbitrary"),
        name="proj_norm" if emit_next else "proj_norm_last",
    )(*args)
    return res if emit_next else (res[0], None, None)


def _latent_kernel(h_ref, r_ref, w_ref, gq_ref, gkv_ref, cos_ref, sin_ref, src_ref,
                   cq_ref, ckv_ref, kr_ref, z_ref, dst_ref, *, chunks):
    dst_ref[...] = src_ref[...].astype(dst_ref.dtype)
    q_end = MLA_Q_LORA
    kv_end = q_end + MLA_KV_LORA
    for rs in _row_chunks(h_ref.shape[0], chunks):
        acc = _dot(h_ref[rs, :], w_ref[...]) * r_ref[rs, 0:1]
        cq = acc[:, :q_end]
        ckv = acc[:, q_end:kv_end]
        cq_ref[rs, :] = (cq * _inv_rms(cq) * gq_ref[...]).astype(cq_ref.dtype)
        ckv_ref[rs, :] = (ckv * _inv_rms(ckv) * gkv_ref[...]).astype(ckv_ref.dtype)
        slab = acc[:, kv_end:kv_end + LANES]
        rot = slab * cos_ref[rs, :] + pltpu.roll(slab, LANES // 2, 1) * sin_ref[rs, :]
        kr_ref[rs, :] = rot.astype(kr_ref.dtype)
        z_ref[rs, :] = acc[:, kv_end + LANES:kv_end + 2 * LANES]


def latent_proj(h, r, w_lat, layer, g_q, g_kv, cos_t, sin_t, side, tm=512, chunks=4):
    m, d = h.shape
    n = LAT_WIDTH
    grid = (m // tm,)
    row = lambda width: pl.BlockSpec((tm, width), lambda i: (i, 0))
    const = lambda c: pl.BlockSpec((1, c), lambda i: (0, 0))
    side_in, side_out, side_shape = _side_cast_specs(side, grid)
    return pl.pallas_call(
        functools.partial(_latent_kernel, chunks=chunks),
        grid=grid,
        in_specs=[row(d), row(LANES), pl.BlockSpec((None, d, n), lambda i: (layer, 0, 0)),
                  const(MLA_Q_LORA), const(MLA_KV_LORA), row(LANES), row(LANES), side_in],
        out_specs=[row(MLA_Q_LORA), row(MLA_KV_LORA), row(LANES), row(LANES), side_out],
        out_shape=[jax.ShapeDtypeStruct((m, MLA_Q_LORA), BF16), jax.ShapeDtypeStruct((m, MLA_KV_LORA), BF16),
                   jax.ShapeDtypeStruct((m, LANES), BF16), jax.ShapeDtypeStruct((m, LANES), F32), side_shape],
        compiler_params=_params("arbitrary"),
        name="latent_proj",
    )(h, r, w_lat, g_q, g_kv, cos_t, sin_t, side.src)


def _q_up_kernel(a_ref, w_ref, cos_ref, sin_ref, o_ref, *, scale):
    acc = _dot(a_ref[...], w_ref[...])
    c = cos_ref[...]
    s = sin_ref[...]
    for h in range(MLA_HEADS):
        base = 2 * LANES * h
        nope = acc[:, base:base + LANES]
        slab = acc[:, base + LANES:base + 2 * LANES]
        rot = slab * c + pltpu.roll(slab, LANES // 2, 1) * s
        o_ref[:, base:base + LANES] = (nope * scale).astype(o_ref.dtype)
        o_ref[:, base + LANES:base + 2 * LANES] = (rot * scale).astype(o_ref.dtype)


def q_up_proj(cqn, w_uq, layer, cos_t, sin_t, tm=1024):
    m, k = cqn.shape
    n = w_uq.shape[2]
    return pl.pallas_call(
        functools.partial(_q_up_kernel, scale=MLA_QK ** -0.5 * LOG2_E),
        grid=(m // tm,),
        in_specs=[pl.BlockSpec((tm, k), lambda i: (i, 0)), pl.BlockSpec((None, k, n), lambda i: (layer, 0, 0)),
                  pl.BlockSpec((tm, LANES), lambda i: (i, 0)), pl.BlockSpec((tm, LANES), lambda i: (i, 0))],
        out_specs=pl.BlockSpec((tm, n), lambda i: (i, 0)),
        out_shape=jax.ShapeDtypeStruct((m, n), BF16),
        compiler_params=_params("parallel"),
        name="q_up_proj",
    )(cqn, w_uq, cos_t, sin_t)


class _Head(NamedTuple):
    q: Any
    q_cols: slice
    k: Any
    k_cols: slice
    v: Any
    v_cols: slice
    o: Any
    o_cols: slice
    fq: Any = None
    fk_row: Any = None


def _causal_blocks(heads, seq, tq):
    row = lax.broadcasted_iota(jnp.int32, (tq, tq), 0)
    col = lax.broadcasted_iota(jnp.int32, (tq, tq), 1)
    tri = col <= row
    for i in range(seq // tq):
        lo, hi = i * tq, (i + 1) * tq
        for hd in heads:
            qi = hd.q[lo:hi, hd.q_cols]
            s_d = _dot_nt(qi, hd.k[lo:hi, hd.k_cols])
            if hd.fk_row is not None:
                s_d = s_d - hd.fk_row[:, lo:hi]
            s_d = jnp.where(tri, s_d, -jnp.inf)
            mx = jnp.max(s_d, axis=-1, keepdims=True)
            if i > 0:
                s_o = _dot_nt(qi, hd.k[:lo, hd.k_cols])
                if hd.fk_row is not None:
                    s_o = s_o - hd.fk_row[:, :lo]
                mx = jnp.maximum(mx, jnp.max(s_o, axis=-1, keepdims=True))
            if hd.fq is not None:
                fq = hd.fq[lo:hi, 0:1]
                off = fq - (mx + fq)
            else:
                off = -mx
            p_d = jnp.exp2(s_d + off)
            den = jnp.sum(p_d, axis=-1, keepdims=True)
            acc = _dot(p_d.astype(BF16), hd.v[lo:hi, hd.v_cols])
            if i > 0:
                p_o = jnp.exp2(s_o + off)
                den = den + jnp.sum(p_o, axis=-1, keepdims=True)
                acc = acc + _dot(p_o.astype(BF16), hd.v[:lo, hd.v_cols])
            hd.o[lo:hi, hd.o_cols] = (acc / den).astype(hd.o.dtype)


def _mla_attn_kernel(q_ref, kv_ref, kr_ref, o_ref, k_scr, *, seq, tq, hps):
    heads = []
    for u in range(hps):
        base = 2 * LANES * u
        k_scr[u, :, :LANES] = kv_ref[:, base:base + LANES]
        k_scr[u, :, LANES:] = kr_ref[...]
        heads.append(_Head(q_ref, slice(base, base + 2 * LANES), k_scr.at[u], slice(None),
                           kv_ref, slice(base + LANES, base + 2 * LANES), o_ref, slice(LANES * u, LANES * (u + 1))))
    _causal_blocks(heads, seq, tq)


def mla_attention(q, kv, k_rot, batch, seq, hps=HEADS_PER_STEP):
    m = q.shape[0]
    return pl.pallas_call(
        functools.partial(_mla_attn_kernel, seq=seq, tq=Q_TILE, hps=hps),
        grid=(batch, MLA_HEADS // hps),
        in_specs=[pl.BlockSpec((seq, 2 * LANES * hps), lambda b, g: (b, g)),
                  pl.BlockSpec((seq, 2 * LANES * hps), lambda b, g: (b, g)),
                  pl.BlockSpec((seq, LANES), lambda b, g: (b, 0))],
        out_specs=pl.BlockSpec((seq, LANES * hps), lambda b, g: (b, g)),
        out_shape=jax.ShapeDtypeStruct((m, MLA_HEADS * LANES), BF16),
        scratch_shapes=[pltpu.VMEM((hps, seq, 2 * LANES), BF16)],
        compiler_params=_params("parallel", "arbitrary"),
        name="mla_attention",
    )(q, kv, k_rot)


def _fox_attn_kernel(q_ref, k_ref, v_ref, ft_ref, o_ref, fq_scr, qx_scr, kx_scr, *, seq, tq, hps):
    g = pl.program_id(1)
    lane = lax.broadcasted_iota(jnp.int32, (seq, LANES), 1)
    ones = jnp.where(lane < 3, 1.0, 0.0).astype(BF16)
    heads = []
    for u in range(hps):
        fk_row = ft_ref[0, pl.ds(g * hps + u, 1), :]
        f = jnp.transpose(jnp.broadcast_to(fk_row, (LANES, seq)))
        fq_scr[u] = f
        hi = f.astype(BF16).astype(F32)
        mid = (f - hi).astype(BF16).astype(F32)
        lo = f - hi - mid
        pieces = jnp.where(lane == 0, hi, jnp.where(lane == 1, mid, jnp.where(lane == 2, lo, 0.0)))
        cols = slice(LANES * u, LANES * (u + 1))
        qx_scr[u, :, :LANES] = q_ref[:, cols]
        qx_scr[u, :, LANES:] = ones
        kx_scr[u, :, :LANES] = k_ref[:, cols]
        kx_scr[u, :, LANES:] = (-pieces).astype(BF16)
        heads.append(_Head(qx_scr.at[u], slice(None), kx_scr.at[u], slice(None), v_ref, cols, o_ref, cols,
                           fq_scr.at[u], None))
    _causal_blocks(heads, seq, tq)


def fox_attention(qkv, f_t, batch, seq, hps=HEADS_PER_STEP):
    m = qkv.shape[0]
    nh = FOX_HEADS
    ng = nh // hps
    blk = lambda part: pl.BlockSpec((seq, LANES * hps), lambda b, g: (b, part * ng + g))
    return pl.pallas_call(
        functools.partial(_fox_attn_kernel, seq=seq, tq=Q_TILE, hps=hps),
        grid=(batch, ng),
        in_specs=[blk(0), blk(1), blk(2), pl.BlockSpec((1, nh, seq), lambda b, g: (b, 0, 0))],
        out_specs=pl.BlockSpec((seq, LANES * hps), lambda b, g: (b, g)),
        out_shape=jax.ShapeDtypeStruct((m, nh * LANES), BF16),
        scratch_shapes=[pltpu.VMEM((hps, seq, LANES), F32), pltpu.VMEM((hps, seq, 2 * LANES), BF16),
                        pltpu.VMEM((hps, seq, 2 * LANES), BF16)],
        compiler_params=_params("parallel", "arbitrary"),
        name="fox_attention",
    )(qkv, qkv, qkv, f_t)


def _fox_decay_kernel(z_ref, b_ref, ft_ref, *, seq):
    x = z_ref[...] + b_ref[...]
    log_f = jnp.minimum(x, 0.0) - jnp.log1p(jnp.exp(-jnp.abs(x)))
    t = jnp.transpose(log_f)[0:FOX_HEADS, :]
    lane = lax.broadcasted_iota(jnp.int32, t.shape, 1)
    shift = 1
    while shift < seq:
        t = t + jnp.where(lane >= shift, pltpu.roll(t, shift, 1), 0.0)
        shift *= 2
    ft_ref[0] = t * LOG2_E


def fox_decay(z_slab, b_row, batch, seq):
    return pl.pallas_call(
        functools.partial(_fox_decay_kernel, seq=seq),
        grid=(batch,),
        in_specs=[pl.BlockSpec((seq, LANES), lambda b: (b, 0)), pl.BlockSpec((1, LANES), lambda b: (0, 0))],
        out_specs=pl.BlockSpec((1, FOX_HEADS, seq), lambda b: (b, 0, 0)),
        out_shape=jax.ShapeDtypeStruct((batch, FOX_HEADS, seq), F32),
        compiler_params=_params("parallel"),
        name="fox_decay",
    )(z_slab, b_row)


def _swa_kernel(slope_ref, sink_ref, q_ref, kp_ref, kc_ref, vp_ref, vc_ref, qpos_ref, pp_ref, pc_ref, o_ref):
    n = pl.program_id(1)
    w = WINDOW
    i = lax.broadcasted_iota(jnp.int32, (w, w), 0)
    j = lax.broadcasted_iota(jnp.int32, (w, w), 1)
    valid = jnp.concatenate([(j > i) & (n > 0), j <= i], axis=1)
    qpos = qpos_ref[:, 0:1]
    dist = jnp.concatenate([qpos - pp_ref[0], qpos - pc_ref[0]], axis=1).astype(F32)
    neg_dist = jnp.where(valid, -dist, -jnp.inf)
    for kh in range(SWA_KV_HEADS):
        ks = slice(kh * HEAD_DIM, (kh + 1) * HEAD_DIM)
        k_win = jnp.concatenate([kp_ref[:, ks], kc_ref[:, ks]], axis=0)
        v_win = jnp.concatenate([vp_ref[:, ks], vc_ref[:, ks]], axis=0)
        for g in range(SWA_GROUP):
            hd = kh * SWA_GROUP + g
            hs = slice(hd * HEAD_DIM, (hd + 1) * HEAD_DIM)
            s = _dot_nt(q_ref[:, hs], k_win) + slope_ref[hd] * neg_dist
            sink = sink_ref[hd]
            mx = jnp.maximum(jnp.max(s, axis=-1, keepdims=True), sink)
            e = jnp.exp2(s - mx)
            den = jnp.sum(e, axis=-1, keepdims=True) + jnp.exp2(sink - mx)
            o_ref[:, hs] = (_dot(e.astype(BF16), v_win) / den).astype(o_ref.dtype)


def swa_attention(qkv, pos_b, pos_row, slopes, sinks, batch, seq):
    m = qkv.shape[0]
    w = WINDOW
    nb = seq // w
    qw = SWA_HEADS * HEAD_DIM
    kvw = SWA_KV_HEADS * HEAD_DIM
    cur = lambda b, n: b * nb + n
    prev = lambda b, n: b * nb + jnp.maximum(n - 1, 0)
    smem = pl.BlockSpec(memory_space=pltpu.SMEM)
    return pl.pallas_call(
        _swa_kernel,
        grid=(batch, nb),
        in_specs=[smem, smem,
                  pl.BlockSpec((w, qw), lambda b, n: (cur(b, n), 0)),
                  pl.BlockSpec((w, kvw), lambda b, n: (prev(b, n), qw // kvw)),
                  pl.BlockSpec((w, kvw), lambda b, n: (cur(b, n), qw // kvw)),
                  pl.BlockSpec((w, kvw), lambda b, n: (prev(b, n), qw // kvw + 1)),
                  pl.BlockSpec((w, kvw), lambda b, n: (cur(b, n), qw // kvw + 1)),
                  pl.BlockSpec((w, LANES), lambda b, n: (cur(b, n), 0)),
                  pl.BlockSpec((1, 1, w), lambda b, n: (prev(b, n), 0, 0)),
                  pl.BlockSpec((1, 1, w), lambda b, n: (cur(b, n), 0, 0))],
        out_specs=pl.BlockSpec((w, qw), lambda b, n: (cur(b, n), 0)),
        out_shape=jax.ShapeDtypeStruct((m, qw), BF16),
        compiler_params=_params("parallel", "arbitrary"),
        name="swa_attention",
    )(slopes, sinks, qkv, qkv, qkv, qkv, qkv, pos_b, pos_row, pos_row)


def _merge_kernel(mla_ref, swa_lo_ref, swa_hi_ref, fox_ref, w0_ref, w1_ref, w2_ref, w3_ref,
                  g0_ref, g1_ref, g2_ref, o_ref, *, chunks):
    for rs in _row_chunks(o_ref.shape[0], chunks):
        acc = g0_ref[rs, :].astype(F32) * _dot(mla_ref[rs, :], w0_ref[...])
        swa = _dot(swa_lo_ref[rs, :], w1_ref[...]) + _dot(swa_hi_ref[rs, :], w2_ref[...])
        acc = acc + g1_ref[rs, :].astype(F32) * swa
        acc = acc + g2_ref[rs, :].astype(F32) * _dot(fox_ref[rs, :], w3_ref[...])
        o_ref[rs, :] = acc.astype(o_ref.dtype)


def gated_merge(o_mla, o_swa, o_fox, w_branch, layer, gates, tm=1024, tn=512, chunks=1):
    m, kb = o_mla.shape
    d = w_branch.shape[2]
    nj = d // tn
    assert o_swa.shape[1] == 2 * kb and o_fox.shape[1] == kb and w_branch.shape[1] == 4 * kb
    a_spec = lambda cb: pl.BlockSpec((tm, kb), lambda i, j: (i, cb))
    w_spec = lambda rb: pl.BlockSpec((None, kb, tn), lambda i, j: (layer, rb, j))
    g_spec = lambda br: pl.BlockSpec((tm, tn), lambda i, j: (i, br * nj + j))
    return pl.pallas_call(
        functools.partial(_merge_kernel, chunks=chunks),
        grid=(m // tm, nj),
        in_specs=[a_spec(0), a_spec(0), a_spec(1), a_spec(0)] + [w_spec(rb) for rb in range(4)]
                 + [g_spec(br) for br in range(3)],
        out_specs=pl.BlockSpec((tm, tn), lambda i, j: (i, j)),
        out_shape=jax.ShapeDtypeStruct((m, d), BF16),
        compiler_params=_params("parallel", "arbitrary"),
        name="gated_merge",
    )(o_mla, o_swa, o_swa, o_fox, w_branch, w_branch, w_branch, w_branch, gates, gates, gates)


def _ffn_up_kernel(h_ref, r_ref, wg_ref, wu_ref, src_ref, o_ref, dst_ref, *, chunks):
    dst_ref[...] = src_ref[...].astype(dst_ref.dtype)
    for rs in _row_chunks(h_ref.shape[0], chunks):
        r = r_ref[rs, 0:1]
        gate = _dot(h_ref[rs, :], wg_ref[...]) * r
        up = _dot(h_ref[rs, :], wu_ref[...]) * r
        o_ref[rs, :] = (gate * _sigmoid(gate) * up).astype(o_ref.dtype)


def ffn_up(h, r, w_gate_up, layer, side, tm=2048, tn=256, chunks=8):
    m, d = h.shape
    n = w_gate_up.shape[2] // 2
    assert n % tn == 0
    nj = n // tn
    grid = (m // tm, nj)
    side_in, side_out, side_shape = _side_cast_specs(side, grid)
    return pl.pallas_call(
        functools.partial(_ffn_up_kernel, chunks=chunks),
        grid=grid,
        in_specs=[pl.BlockSpec((tm, d), lambda i, j: (i, 0)),
                  pl.BlockSpec((tm, LANES), lambda i, j: (i, 0)),
                  pl.BlockSpec((None, d, tn), lambda i, j: (layer, 0, j)),
                  pl.BlockSpec((None, d, tn), lambda i, j: (layer, 0, nj + j)),
                  side_in],
        out_specs=[pl.BlockSpec((tm, tn), lambda i, j: (i, j)), side_out],
        out_shape=[jax.ShapeDtypeStruct((m, n), BF16), side_shape],
        compiler_params=_params("arbitrary", "arbitrary"),
        name="ffn_up",
    )(h, r, w_gate_up, w_gate_up, side.src)


def _regroup_kernel(main_ref, next_ref, o_ref, *, shift):
    x = jnp.concatenate([main_ref[shift:, :], next_ref[:shift, :]], axis=0)
    o_ref[...] = jnp.transpose(x).astype(o_ref.dtype)


def regroup_cast(w_t, row0, width, tile=512, tk=2048):
    depth, _, k = w_t.shape
    shift = row0 % LANES
    base = row0 - shift
    assert shift % 8 == 0 and shift and base % tile == 0 and width % tile == 0 and k % tk == 0
    return pl.pallas_call(
        functools.partial(_regroup_kernel, shift=shift),
        grid=(depth, k // tk, width // tile),
        in_specs=[pl.BlockSpec((None, tile, tk), lambda l, kb, c: (l, base // tile + c, kb)),
                  pl.BlockSpec((None, LANES, tk), lambda l, kb, c: (l, (base + (c + 1) * tile) // LANES, kb))],
        out_specs=pl.BlockSpec((None, tk, tile), lambda l, kb, c: (l, kb, c)),
        out_shape=jax.ShapeDtypeStruct((depth, k, width), BF16),
        compiler_params=_params("parallel", "parallel", "arbitrary"),
        name="regroup_cast",
    )(w_t, w_t)


def _latent_weight_kernel(main_ref, kr_ref, z_ref, o_ref):
    half = MLA_ROPE // 2
    kr = kr_ref[...]
    pad = jnp.zeros((LANES - FOX_HEADS, kr.shape[1]), kr.dtype)
    tail = jnp.concatenate([kr, kr[half:, :], kr[:half, :], z_ref[...], pad], axis=0)
    n_main = main_ref.shape[0]
    o_ref[:, :n_main] = jnp.transpose(main_ref[...]).astype(o_ref.dtype)
    o_ref[:, n_main:] = jnp.transpose(tail).astype(o_ref.dtype)


def latent_weights(w_t, kr_row, z_row, tk=1024):
    depth, _, k = w_t.shape
    n_main = kr_row
    assert n_main % LANES == 0 and kr_row % MLA_ROPE == 0 and z_row % FOX_HEADS == 0 and FOX_HEADS % 8 == 0
    return pl.pallas_call(
        _latent_weight_kernel,
        grid=(depth, k // tk),
        in_specs=[pl.BlockSpec((None, n_main, tk), lambda l, kb: (l, 0, kb)),
                  pl.BlockSpec((None, MLA_ROPE, tk), lambda l, kb: (l, kr_row // MLA_ROPE, kb)),
                  pl.BlockSpec((None, FOX_HEADS, tk), lambda l, kb: (l, z_row // FOX_HEADS, kb))],
        out_specs=pl.BlockSpec((None, tk, LAT_WIDTH), lambda l, kb: (l, kb, 0)),
        out_shape=jax.ShapeDtypeStruct((depth, k, LAT_WIDTH), BF16),
        compiler_params=_params("parallel", "parallel"),
        name="latent_weights",
    )(w_t, w_t, w_t)


def _input_proj_weights(w_in):
    d = w_in.shape[1]
    w_t = jnp.swapaxes(w_in, 1, 2)
    o = 0
    cuts = {}
    for name, width in (("cq", MLA_Q_LORA), ("ckv", MLA_KV_LORA), ("kr", MLA_ROPE), ("swa", SWA_WIDTH),
                        ("fox", FOX_WIDTH), ("z", FOX_HEADS), ("gate", 3 * d)):
        cuts[name] = (o, o + width)
        o += width
    assert cuts["cq"][0] == 0 and cuts["kr"][0] == MLA_Q_LORA + MLA_KV_LORA
    w_lat = latent_weights(w_t, cuts["kr"][0], cuts["z"][0])
    w_gate = regroup_cast(w_t, cuts["gate"][0], 3 * d)
    w_fox = regroup_cast(w_t, cuts["fox"][0], FOX_WIDTH)
    w_swa = regroup_cast(w_t, cuts["swa"][0], SWA_WIDTH)
    return w_gate, w_fox, w_swa, w_lat


def _q_up_weights(w_uq):
    depth = w_uq.shape[0]
    half = MLA_ROPE // 2
    uq = w_uq.reshape(depth, MLA_Q_LORA, MLA_HEADS, MLA_QK)
    rope = uq[..., MLA_NOPE:]
    rope_swapped = jnp.concatenate([rope[..., half:], rope[..., :half]], axis=-1)
    w_q = jnp.concatenate([uq[..., :MLA_NOPE], rope, rope_swapped], axis=-1)
    return w_q.reshape(depth, MLA_Q_LORA, MLA_HEADS * 2 * LANES).astype(BF16)


def _rope_consts():
    half = MLA_ROPE // 2
    inv_freq = ROPE_THETA ** (-jnp.arange(half, dtype=F32) / half)
    zeros = jnp.zeros((LANES - MLA_ROPE,), F32)
    ones = jnp.ones((half,), F32)
    rows = jnp.stack([jnp.concatenate([inv_freq, inv_freq, zeros]),
                      jnp.concatenate([ones, ones, zeros]),
                      jnp.concatenate([-ones, ones, zeros])])
    return jnp.concatenate([rows, jnp.zeros((5, LANES), F32)], axis=0)


def kernel(x, positions, g_mix_pre, g_mix_post, g_ffn_pre, g_ffn_post, w_in, g_q_lora, w_uq, g_kv_lora, w_ukv,
           b_forget, swa_sinks, w_branch, w_out, w_gate_up, w_down):
    batch, seq, d = x.shape
    depth = w_in.shape[0]
    m = batch * seq
    xs = x.reshape(m, d)
    pos_b = jnp.broadcast_to(positions.reshape(m, 1), (m, LANES))
    pos_row = positions.reshape(m // WINDOW, 1, WINDOW)
    slopes = 2.0 ** (-8.0 * jnp.arange(1, SWA_HEADS + 1, dtype=F32) / SWA_HEADS)
    cos_t, sin_t = rope_tables(pos_b, _rope_consts())
    vec = lambda g: g.reshape(1, -1)

    w_gate, w_fox, w_swa, w_lat = _input_proj_weights(w_in)
    w_q = _q_up_weights(w_uq)
    w_kv = w_ukv.astype(BF16)
    b_rows = jnp.zeros((depth, 1, LANES), F32).at[:, 0, :FOX_HEADS].set(b_forget)
    q_scale = HEAD_DIM ** -0.5 * LOG2_E
    qkv_scale = lambda q_width, width: jnp.concatenate(
        [jnp.full((1, q_width), q_scale, F32), jnp.ones((1, width - q_width), F32)], axis=1)
    fox_scale = qkv_scale(FOX_HEADS * HEAD_DIM, FOX_WIDTH)
    swa_scale = qkv_scale(SWA_HEADS * HEAD_DIM, SWA_WIDTH)

    h, r = prenorm(xs, vec(g_mix_pre[0]))
    for l in range(depth):
        gates, w_gu = matmul(h, w_gate, l, 3 * d, BF16, 1024, 1024, row_scale=r, act="sigmoid", chunks=1,
                             side=SideCast(w_gate_up, l), name="gate_proj")
        qkv_fox, w_br = matmul(h, w_fox, l, FOX_WIDTH, BF16, 1024, 1024, row_scale=r, col_scale=fox_scale,
                               chunks=1, side=SideCast(w_branch, l), name="fox_proj")
        qkv_swa = matmul(h, w_swa, l, SWA_WIDTH, BF16, 1024, SWA_WIDTH // 2, row_scale=r, col_scale=swa_scale,
                         chunks=1, name="swa_proj")
        cqn, ckvn, k_rot, z_slab, w_o = latent_proj(h, r, w_lat, l, vec(g_q_lora[l]), vec(g_kv_lora[l]),
                                                    cos_t, sin_t, SideCast(w_out, l))
        q_mla = q_up_proj(cqn, w_q, l, cos_t, sin_t)
        kv_mla = matmul(ckvn, w_kv, l, w_kv.shape[2], BF16, 1024, 1024, name="kv_up_proj")
        f_t = fox_decay(z_slab, b_rows[l], batch, seq)
        o_mla = mla_attention(q_mla, kv_mla, k_rot, batch, seq)
        o_swa = swa_attention(qkv_swa, pos_b, pos_row, slopes * LOG2_E, swa_sinks[l].astype(F32) * LOG2_E,
                              batch, seq)
        o_fox = fox_attention(qkv_fox, f_t, batch, seq)
        merged = gated_merge(o_mla, o_swa, o_fox, w_br, 0, gates)
        xs, h2, r2 = proj_norm(merged, w_o, 0, xs, vec(g_mix_post[l]), vec(g_ffn_pre[l]), 512, 1024)
        act, w_dn = ffn_up(h2, r2, w_gu, 0, SideCast(w_down, l))
        g_next = vec(g_mix_pre[l + 1]) if l + 1 < depth else None
        xs, h, r = proj_norm(act, w_dn, 0, xs, vec(g_ffn_post[l]), g_next, 512, 256)
    return xs.reshape(batch, seq, d)
```

```python
import functools
import math
from typing import Any, NamedTuple

import jax
import jax.numpy as jnp
from jax import lax
from jax.experimental import pallas as pl
from jax.experimental.pallas import tpu as pltpu

F32 = jnp.float32
BF16 = jnp.bfloat16

VMEM_LIMIT_BYTES = 56 * 1024 * 1024
LANES = 128

HEAD_DIM = 128
MLA_HEADS = 8
MLA_Q_LORA = 1024
MLA_KV_LORA = 512
MLA_NOPE = 128
MLA_ROPE = 64
MLA_QK = MLA_NOPE + MLA_ROPE
SWA_HEADS = 16
SWA_KV_HEADS = 2
SWA_GROUP = SWA_HEADS // SWA_KV_HEADS
WINDOW = 128
FOX_HEADS = 8
ROPE_THETA = 10000.0
RMS_EPS = 1e-6
LOG2_E = math.log2(math.e)
Q_TILE = 512
HEADS_PER_STEP = 2

SWA_WIDTH = (SWA_HEADS + 2 * SWA_KV_HEADS) * HEAD_DIM
FOX_WIDTH = 3 * FOX_HEADS * HEAD_DIM
LAT_WIDTH = MLA_Q_LORA + MLA_KV_LORA + 2 * LANES


def _params(*sem):
    return pltpu.CompilerParams(dimension_semantics=sem, vmem_limit_bytes=VMEM_LIMIT_BYTES)


def _inv_rms(x):
    return lax.rsqrt(jnp.mean(x * x, axis=-1, keepdims=True) + RMS_EPS)


def _dot(a, b):
    return jnp.dot(a, b, preferred_element_type=F32)


def _dot_nt(a, b):
    return lax.dot_general(a, b, (((1,), (1,)), ((), ())), preferred_element_type=F32)


def _sigmoid(x):
    return 0.5 * jnp.tanh(0.5 * x) + 0.5


def _row_chunks(rows, chunks):
    step = rows // chunks
    return [slice(c * step, (c + 1) * step) for c in range(chunks)]


def _rope_table_kernel(pos_ref, c_ref, cos_ref, sin_ref):
    ang = pos_ref[...].astype(F32) * c_ref[0:1, :]
    cos_ref[...] = jnp.cos(ang) * c_ref[1:2, :]
    sin_ref[...] = jnp.sin(ang) * c_ref[2:3, :]


def rope_tables(pos_b, consts, tm=2048):
    m = pos_b.shape[0]
    spec = pl.BlockSpec((tm, LANES), lambda i: (i, 0))
    return pl.pallas_call(
        _rope_table_kernel,
        grid=(m // tm,),
        in_specs=[spec, pl.BlockSpec((8, LANES), lambda i: (0, 0))],
        out_specs=[spec, spec],
        out_shape=[jax.ShapeDtypeStruct((m, LANES), F32)] * 2,
        compiler_params=_params("parallel"),
        name="rope_tables",
    )(pos_b, consts)


def _prenorm_kernel(x_ref, g_ref, h_ref, r_ref):
    x = x_ref[...]
    h_ref[...] = (x * g_ref[...]).astype(h_ref.dtype)
    r_ref[...] = jnp.broadcast_to(_inv_rms(x), r_ref.shape)


def prenorm(x, gain, tm=256):
    m, d = x.shape
    return pl.pallas_call(
        _prenorm_kernel,
        grid=(m // tm,),
        in_specs=[pl.BlockSpec((tm, d), lambda i: (i, 0)), pl.BlockSpec((1, d), lambda i: (0, 0))],
        out_specs=[pl.BlockSpec((tm, d), lambda i: (i, 0)), pl.BlockSpec((tm, LANES), lambda i: (i, 0))],
        out_shape=[jax.ShapeDtypeStruct((m, d), BF16), jax.ShapeDtypeStruct((m, LANES), F32)],
        compiler_params=_params("parallel"),
        name="prenorm",
    )(x, gain)


class SideCast(NamedTuple):
    src: Any
    layer: int


BF16_SUBLANES = 16


def _side_cast_specs(side, grid):
    _, rows, cols = side.src.shape
    steps = math.prod(grid)
    n_blocks = max(nb for nb in range(1, steps + 1) if rows % (nb * BF16_SUBLANES) == 0)
    br = rows // n_blocks

    def row_block(*ids):
        step = 0
        for extent, idx in zip(grid, ids):
            step = step * extent + idx
        return jnp.minimum(step, n_blocks - 1)

    in_spec = pl.BlockSpec((None, br, cols), lambda *ids: (side.layer, row_block(*ids), 0))
    out_spec = pl.BlockSpec((None, br, cols), lambda *ids: (0, row_block(*ids), 0))
    return in_spec, out_spec, jax.ShapeDtypeStruct((1, rows, cols), BF16)


def _matmul_kernel(*refs, act, row_scaled, col_scaled, side, chunks):
    refs = list(refs)
    a_ref = refs.pop(0)
    r_ref = refs.pop(0) if row_scaled else None
    c_ref = refs.pop(0) if col_scaled else None
    w_ref = refs.pop(0)
    if side:
        src_ref, o_ref, dst_ref = refs
        dst_ref[...] = src_ref[...].astype(dst_ref.dtype)
    else:
        (o_ref,) = refs
    for rs in _row_chunks(a_ref.shape[0], chunks):
        acc = _dot(a_ref[rs, :], w_ref[...])
        if row_scaled:
            acc = acc * r_ref[rs, 0:1]
        if col_scaled:
            acc = acc * c_ref[...]
        if act == "sigmoid":
            acc = _sigmoid(acc)
        o_ref[rs, :] = acc.astype(o_ref.dtype)


def matmul(a, w, layer, n, out_dtype, tm, tn, row_scale=None, col_scale=None, act=None, side=None, chunks=1,
           name="matmul"):
    m, k = a.shape
    assert m % tm == 0 and n % tn == 0 and w.shape[1:] == (k, n)
    grid = (m // tm, n // tn)
    in_specs = [pl.BlockSpec((tm, k), lambda i, j: (i, 0))]
    args = [a]
    if row_scale is not None:
        in_specs.append(pl.BlockSpec((tm, LANES), lambda i, j: (i, 0)))
        args.append(row_scale)
    if col_scale is not None:
        in_specs.append(pl.BlockSpec((1, tn), lambda i, j: (0, j)))
        args.append(col_scale)
    in_specs.append(pl.BlockSpec((None, k, tn), lambda i, j: (layer, 0, j)))
    args.append(w)
    out_specs = [pl.BlockSpec((tm, tn), lambda i, j: (i, j))]
    out_shape = [jax.ShapeDtypeStruct((m, n), out_dtype)]
    if side is not None:
        side_in, side_out, side_shape = _side_cast_specs(side, grid)
        in_specs.append(side_in)
        args.append(side.src)
        out_specs.append(side_out)
        out_shape.append(side_shape)
    res = pl.pallas_call(
        functools.partial(_matmul_kernel, act=act, row_scaled=row_scale is not None,
                          col_scaled=col_scale is not None, side=side is not None, chunks=chunks),
        grid=grid,
        in_specs=in_specs,
        out_specs=out_specs,
        out_shape=out_shape,
        compiler_params=_params("arbitrary", "arbitrary"),
        name=name,
    )(*args)
    return res if side is not None else res[0]


def _proj_norm_kernel(*refs, n_i, n_j, d, emit_next):
    if emit_next:
        a_ref, w_ref, x_ref, gp_ref, gn_ref, xo_ref, h_ref, r_ref, y_scr, acc_scr, done_scr, acc2_scr = refs
    else:
        a_ref, w_ref, x_ref, gp_ref, xo_ref, y_scr, acc_scr, done_scr = refs
    i = pl.program_id(0)
    j = pl.program_id(1)

    @pl.when((i == 0) & (j == 0))
    def _():
        y_scr[...] = jnp.zeros_like(y_scr)
        acc_scr[...] = jnp.zeros_like(acc_scr)
        done_scr[...] = jnp.zeros_like(done_scr)
        if emit_next:
            acc2_scr[...] = jnp.zeros_like(acc2_scr)

    first = j == 0
    y_prev = y_scr[j]
    y_new = _dot(a_ref[...], w_ref[...])
    y_scr[j] = y_new
    part = jnp.sum(y_new * y_new, axis=-1, keepdims=True)
    acc_old = acc_scr[...]
    done = jnp.where(first, acc_old, done_scr[...])
    done_scr[...] = done
    acc_scr[...] = jnp.where(first, part, acc_old + part)

    x_new = x_ref[...] + y_prev * lax.rsqrt(done * (1.0 / d) + RMS_EPS) * gp_ref[...]
    xo_ref[...] = x_new
    if emit_next:
        h_ref[...] = (x_new * gn_ref[...]).astype(h_ref.dtype)
        part2 = jnp.sum(x_new * x_new, axis=-1, keepdims=True)
        acc2 = jnp.where(first, part2, acc2_scr[...] + part2)
        acc2_scr[...] = acc2
        r_ref[...] = jnp.broadcast_to(lax.rsqrt(acc2 * (1.0 / d) + RMS_EPS), r_ref.shape)


def proj_norm(a, w, layer, x, g_post, g_next, tm, tn):
    m, k = a.shape
    d = w.shape[2]
    n_i, n_j = m // tm, d // tn
    emit_next = g_next is not None
    prev = lambda i: jnp.maximum(i - 1, 0)
    col = pl.BlockSpec((tm, tn), lambda i, j: (prev(i), j))
    col_out = pl.BlockSpec((tm, tn), lambda i, j: (prev(i), jnp.where(i == 0, 0, j)))
    vec = pl.BlockSpec((1, tn), lambda i, j: (0, j))
    in_specs = [pl.BlockSpec((tm, k), lambda i, j: (jnp.minimum(i, n_i - 1), 0)),
                pl.BlockSpec((None, k, tn), lambda i, j: (layer, 0, j)),
                col, vec]
    args = [a, w, x, g_post]
    out_specs = [col_out]
    out_shape = [jax.ShapeDtypeStruct((m, d), F32)]
    scratch = [pltpu.VMEM((n_j, tm, tn), F32), pltpu.VMEM((tm, 1), F32), pltpu.VMEM((tm, 1), F32)]
    if emit_next:
        in_specs.append(vec)
        args.append(g_next)
        out_specs += [col_out, pl.BlockSpec((tm, LANES), lambda i, j: (prev(i), 0))]
        out_shape += [jax.ShapeDtypeStruct((m, d), BF16), jax.ShapeDtypeStruct((m, LANES), F32)]
        scratch.append(pltpu.VMEM((tm, 1), F32))
    res = pl.pallas_call(
        functools.partial(_proj_norm_kernel, n_i=n_i, n_j=n_j, d=d, emit_next=emit_next),
        grid=(n_i + 1, n_j),
        in_specs=in_specs,
        out_specs=out_specs,
        out_shape=out_shape,
        scratch_shapes=scratch,
        compiler_params=_params("arbitrary", "arbitrary"),
        name="proj_norm" if emit_next else "proj_norm_last",
    )(*args)
    return res if emit_next else (res[0], None, None)


def _latent_kernel(h_ref, r_ref, w_ref, gq_ref, gkv_ref, cos_ref, sin_ref, src_ref,
                   cq_ref, ckv_ref, kr_ref, z_ref, dst_ref, *, chunks):
    dst_ref[...] = src_ref[...].astype(dst_ref.dtype)
    q_end = MLA_Q_LORA
    kv_end = q_end + MLA_KV_LORA
    for rs in _row_chunks(h_ref.shape[0], chunks):
        acc = _dot(h_ref[rs, :], w_ref[...]) * r_ref[rs, 0:1]
        cq = acc[:, :q_end]
        ckv = acc[:, q_end:kv_end]
        cq_ref[rs, :] = (cq * _inv_rms(cq) * gq_ref[...]).astype(cq_ref.dtype)
        ckv_ref[rs, :] = (ckv * _inv_rms(ckv) * gkv_ref[...]).astype(ckv_ref.dtype)
        slab = acc[:, kv_end:kv_end + LANES]
        rot = slab * cos_ref[rs, :] + pltpu.roll(slab, LANES // 2, 1) * sin_ref[rs, :]
        kr_ref[rs, :] = rot.astype(kr_ref.dtype)
        z_ref[rs, :] = acc[:, kv_end + LANES:kv_end + 2 * LANES]


def latent_proj(h, r, w_lat, layer, g_q, g_kv, cos_t, sin_t, side, tm=512, chunks=4):
    m, d = h.shape
    n = LAT_WIDTH
    grid = (m // tm,)
    row = lambda width: pl.BlockSpec((tm, width), lambda i: (i, 0))
    const = lambda c: pl.BlockSpec((1, c), lambda i: (0, 0))
    side_in, side_out, side_shape = _side_cast_specs(side, grid)
    return pl.pallas_call(
        functools.partial(_latent_kernel, chunks=chunks),
        grid=grid,
        in_specs=[row(d), row(LANES), pl.BlockSpec((None, d, n), lambda i: (layer, 0, 0)),
                  const(MLA_Q_LORA), const(MLA_KV_LORA), row(LANES), row(LANES), side_in],
        out_specs=[row(MLA_Q_LORA), row(MLA_KV_LORA), row(LANES), row(LANES), side_out],
        out_shape=[jax.ShapeDtypeStruct((m, MLA_Q_LORA), BF16), jax.ShapeDtypeStruct((m, MLA_KV_LORA), BF16),
                   jax.ShapeDtypeStruct((m, LANES), BF16), jax.ShapeDtypeStruct((m, LANES), F32), side_shape],
        compiler_params=_params("arbitrary"),
        name="latent_proj",
    )(h, r, w_lat, g_q, g_kv, cos_t, sin_t, side.src)


def _q_up_kernel(a_ref, w_ref, cos_ref, sin_ref, o_ref, *, scale):
    acc = _dot(a_ref[...], w_ref[...])
    c = cos_ref[...]
    s = sin_ref[...]
    for h in range(MLA_HEADS):
        base = 2 * LANES * h
        nope = acc[:, base:base + LANES]
        slab = acc[:, base + LANES:base + 2 * LANES]
        rot = slab * c + pltpu.roll(slab, LANES // 2, 1) * s
        o_ref[:, base:base + LANES] = (nope * scale).astype(o_ref.dtype)
        o_ref[:, base + LANES:base + 2 * LANES] = (rot * scale).astype(o_ref.dtype)


def q_up_proj(cqn, w_uq, layer, cos_t, sin_t, tm=1024):
    m, k = cqn.shape
    n = w_uq.shape[2]
    return pl.pallas_call(
        functools.partial(_q_up_kernel, scale=MLA_QK ** -0.5 * LOG2_E),
        grid=(m // tm,),
        in_specs=[pl.BlockSpec((tm, k), lambda i: (i, 0)), pl.BlockSpec((None, k, n), lambda i: (layer, 0, 0)),
                  pl.BlockSpec((tm, LANES), lambda i: (i, 0)), pl.BlockSpec((tm, LANES), lambda i: (i, 0))],
        out_specs=pl.BlockSpec((tm, n), lambda i: (i, 0)),
        out_shape=jax.ShapeDtypeStruct((m, n), BF16),
        compiler_params=_params("parallel"),
        name="q_up_proj",
    )(cqn, w_uq, cos_t, sin_t)


class _Head(NamedTuple):
    q: Any
    q_cols: slice
    k: Any
    k_cols: slice
    v: Any
    v_cols: slice
    o: Any
    o_cols: slice
    fq: Any = None
    fk_row: Any = None


def _causal_blocks(heads, seq, tq, longest_first):
    row = lax.broadcasted_iota(jnp.int32, (tq, tq), 0)
    col = lax.broadcasted_iota(jnp.int32, (tq, tq), 1)
    tri = col <= row
    tiles = range(seq // tq)
    for i in (reversed(tiles) if longest_first else tiles):
        lo, hi = i * tq, (i + 1) * tq
        for hd in heads:
            qi = hd.q[lo:hi, hd.q_cols]
            s_d = _dot_nt(qi, hd.k[lo:hi, hd.k_cols])
            if hd.fk_row is not None:
                s_d = s_d - hd.fk_row[:, lo:hi]
            s_d = jnp.where(tri, s_d, -jnp.inf)
            mx = jnp.max(s_d, axis=-1, keepdims=True)
            if i > 0:
                s_o = _dot_nt(qi, hd.k[:lo, hd.k_cols])
                if hd.fk_row is not None:
                    s_o = s_o - hd.fk_row[:, :lo]
                mx = jnp.maximum(mx, jnp.max(s_o, axis=-1, keepdims=True))
            if hd.fq is not None:
                fq = hd.fq[lo:hi, 0:1]
                off = fq - (mx + fq)
            else:
                off = -mx
            p_d = jnp.exp2(s_d + off)
            den = jnp.sum(p_d, axis=-1, keepdims=True)
            acc = _dot(p_d.astype(BF16), hd.v[lo:hi, hd.v_cols])
            if i > 0:
                p_o = jnp.exp2(s_o + off)
                den = den + jnp.sum(p_o, axis=-1, keepdims=True)
                acc = acc + _dot(p_o.astype(BF16), hd.v[:lo, hd.v_cols])
            hd.o[lo:hi, hd.o_cols] = (acc / den).astype(hd.o.dtype)


def _mla_attn_kernel(q_ref, kv_ref, kr_ref, o_ref, k_scr, *, seq, tq, hps):
    heads = []
    for u in range(hps):
        base = 2 * LANES * u
        k_scr[u, :, :LANES] = kv_ref[:, base:base + LANES]
        k_scr[u, :, LANES:] = kr_ref[...]
        heads.append(_Head(q_ref, slice(base, base + 2 * LANES), k_scr.at[u], slice(None),
                           kv_ref, slice(base + LANES, base + 2 * LANES), o_ref, slice(LANES * u, LANES * (u + 1))))
    _causal_blocks(heads, seq, tq, longest_first=True)


def mla_attention(q, kv, k_rot, batch, seq, hps=HEADS_PER_STEP):
    m = q.shape[0]
    return pl.pallas_call(
        functools.partial(_mla_attn_kernel, seq=seq, tq=Q_TILE, hps=hps),
        grid=(batch, MLA_HEADS // hps),
        in_specs=[pl.BlockSpec((seq, 2 * LANES * hps), lambda b, g: (b, g)),
                  pl.BlockSpec((seq, 2 * LANES * hps), lambda b, g: (b, g)),
                  pl.BlockSpec((seq, LANES), lambda b, g: (b, 0))],
        out_specs=pl.BlockSpec((seq, LANES * hps), lambda b, g: (b, g)),
        out_shape=jax.ShapeDtypeStruct((m, MLA_HEADS * LANES), BF16),
        scratch_shapes=[pltpu.VMEM((hps, seq, 2 * LANES), BF16)],
        compiler_params=_params("parallel", "arbitrary"),
        name="mla_attention",
    )(q, kv, k_rot)


def _fox_attn_kernel(q_ref, k_ref, v_ref, ft_ref, o_ref, fq_scr, qx_scr, kx_scr, *, seq, tq, hps):
    g = pl.program_id(1)
    lane = lax.broadcasted_iota(jnp.int32, (seq, LANES), 1)
    ones = jnp.where(lane < 3, 1.0, 0.0).astype(BF16)
    heads = []
    for u in range(hps):
        fk_row = ft_ref[0, pl.ds(g * hps + u, 1), :]
        f = jnp.transpose(jnp.broadcast_to(fk_row, (LANES, seq)))
        fq_scr[u] = f
        hi = f.astype(BF16).astype(F32)
        mid = (f - hi).astype(BF16).astype(F32)
        lo = f - hi - mid
        pieces = jnp.where(lane == 0, hi, jnp.where(lane == 1, mid, jnp.where(lane == 2, lo, 0.0)))
        cols = slice(LANES * u, LANES * (u + 1))
        qx_scr[u, :, :LANES] = q_ref[:, cols]
        qx_scr[u, :, LANES:] = ones
        kx_scr[u, :, :LANES] = k_ref[:, cols]
        kx_scr[u, :, LANES:] = (-pieces).astype(BF16)
        heads.append(_Head(qx_scr.at[u], slice(None), kx_scr.at[u], slice(None), v_ref, cols, o_ref, cols,
                           fq_scr.at[u], None))
    _causal_blocks(heads, seq, tq, longest_first=False)


def fox_attention(qkv, f_t, batch, seq, hps=HEADS_PER_STEP):
    m = qkv.shape[0]
    nh = FOX_HEADS
    ng = nh // hps
    blk = lambda part: pl.BlockSpec((seq, LANES * hps), lambda b, g: (b, part * ng + g))
    return pl.pallas_call(
        functools.partial(_fox_attn_kernel, seq=seq, tq=Q_TILE, hps=hps),
        grid=(batch, ng),
        in_specs=[blk(0), blk(1), blk(2), pl.BlockSpec((1, nh, seq), lambda b, g: (b, 0, 0))],
        out_specs=pl.BlockSpec((seq, LANES * hps), lambda b, g: (b, g)),
        out_shape=jax.ShapeDtypeStruct((m, nh * LANES), BF16),
        scratch_shapes=[pltpu.VMEM((hps, seq, LANES), F32), pltpu.VMEM((hps, seq, 2 * LANES), BF16),
                        pltpu.VMEM((hps, seq, 2 * LANES), BF16)],
        compiler_params=_params("parallel", "arbitrary"),
        name="fox_attention",
    )(qkv, qkv, qkv, f_t)


def _fox_decay_kernel(z_ref, b_ref, ft_ref, *, seq):
    x = z_ref[...] + b_ref[...]
    log_f = jnp.minimum(x, 0.0) - jnp.log1p(jnp.exp(-jnp.abs(x)))
    t = jnp.transpose(log_f)[0:FOX_HEADS, :]
    lane = lax.broadcasted_iota(jnp.int32, t.shape, 1)
    shift = 1
    while shift < seq:
        t = t + jnp.where(lane >= shift, pltpu.roll(t, shift, 1), 0.0)
        shift *= 2
    ft_ref[0] = t * LOG2_E


def fox_decay(z_slab, b_row, batch, seq):
    return pl.pallas_call(
        functools.partial(_fox_decay_kernel, seq=seq),
        grid=(batch,),
        in_specs=[pl.BlockSpec((seq, LANES), lambda b: (b, 0)), pl.BlockSpec((1, LANES), lambda b: (0, 0))],
        out_specs=pl.BlockSpec((1, FOX_HEADS, seq), lambda b: (b, 0, 0)),
        out_shape=jax.ShapeDtypeStruct((batch, FOX_HEADS, seq), F32),
        compiler_params=_params("parallel"),
        name="fox_decay",
    )(z_slab, b_row)


def _swa_kernel(slope_ref, sink_ref, q_ref, kp_ref, kc_ref, vp_ref, vc_ref, qpos_ref, pp_ref, pc_ref, o_ref):
    n = pl.program_id(1)
    w = WINDOW
    i = lax.broadcasted_iota(jnp.int32, (w, w), 0)
    j = lax.broadcasted_iota(jnp.int32, (w, w), 1)
    valid = jnp.concatenate([(j > i) & (n > 0), j <= i], axis=1)
    qpos = qpos_ref[:, 0:1]
    dist = jnp.concatenate([qpos - pp_ref[0], qpos - pc_ref[0]], axis=1).astype(F32)
    neg_dist = jnp.where(valid, -dist, -jnp.inf)
    for kh in range(SWA_KV_HEADS):
        ks = slice(kh * HEAD_DIM, (kh + 1) * HEAD_DIM)
        k_win = jnp.concatenate([kp_ref[:, ks], kc_ref[:, ks]], axis=0)
        v_win = jnp.concatenate([vp_ref[:, ks], vc_ref[:, ks]], axis=0)
        for g in range(SWA_GROUP):
            hd = kh * SWA_GROUP + g
            hs = slice(hd * HEAD_DIM, (hd + 1) * HEAD_DIM)
            s = _dot_nt(q_ref[:, hs], k_win) + slope_ref[hd] * neg_dist
            sink = sink_ref[hd]
            mx = jnp.maximum(jnp.max(s, axis=-1, keepdims=True), sink)
            e = jnp.exp2(s - mx)
            den = jnp.sum(e, axis=-1, keepdims=True) + jnp.exp2(sink - mx)
            o_ref[:, hs] = (_dot(e.astype(BF16), v_win) / den).astype(o_ref.dtype)


def swa_attention(qkv, pos_b, pos_row, slopes, sinks, batch, seq):
    m = qkv.shape[0]
    w = WINDOW
    nb = seq // w
    qw = SWA_HEADS * HEAD_DIM
    kvw = SWA_KV_HEADS * HEAD_DIM
    cur = lambda b, n: b * nb + n
    prev = lambda b, n: b * nb + jnp.maximum(n - 1, 0)
    smem = pl.BlockSpec(memory_space=pltpu.SMEM)
    return pl.pallas_call(
        _swa_kernel,
        grid=(batch, nb),
        in_specs=[smem, smem,
                  pl.BlockSpec((w, qw), lambda b, n: (cur(b, n), 0)),
                  pl.BlockSpec((w, kvw), lambda b, n: (prev(b, n), qw // kvw)),
                  pl.BlockSpec((w, kvw), lambda b, n: (cur(b, n), qw // kvw)),
                  pl.BlockSpec((w, kvw), lambda b, n: (prev(b, n), qw // kvw + 1)),
                  pl.BlockSpec((w, kvw), lambda b, n: (cur(b, n), qw // kvw + 1)),
                  pl.BlockSpec((w, LANES), lambda b, n: (cur(b, n), 0)),
                  pl.BlockSpec((1, 1, w), lambda b, n: (prev(b, n), 0, 0)),
                  pl.BlockSpec((1, 1, w), lambda b, n: (cur(b, n), 0, 0))],
        out_specs=pl.BlockSpec((w, qw), lambda b, n: (cur(b, n), 0)),
        out_shape=jax.ShapeDtypeStruct((m, qw), BF16),
        compiler_params=_params("parallel", "arbitrary"),
        name="swa_attention",
    )(slopes, sinks, qkv, qkv, qkv, qkv, qkv, pos_b, pos_row, pos_row)


def _merge_kernel(mla_ref, swa_lo_ref, swa_hi_ref, fox_ref, w0_ref, w1_ref, w2_ref, w3_ref,
                  g0_ref, g1_ref, g2_ref, o_ref, *, chunks):
    for rs in _row_chunks(o_ref.shape[0], chunks):
        acc = g0_ref[rs, :].astype(F32) * _dot(mla_ref[rs, :], w0_ref[...])
        swa = _dot(swa_lo_ref[rs, :], w1_ref[...]) + _dot(swa_hi_ref[rs, :], w2_ref[...])
        acc = acc + g1_ref[rs, :].astype(F32) * swa
        acc = acc + g2_ref[rs, :].astype(F32) * _dot(fox_ref[rs, :], w3_ref[...])
        o_ref[rs, :] = acc.astype(o_ref.dtype)


def gated_merge(o_mla, o_swa, o_fox, w_branch, layer, gates, tm=1024, tn=512, chunks=1):
    m, kb = o_mla.shape
    d = w_branch.shape[2]
    nj = d // tn
    assert o_swa.shape[1] == 2 * kb and o_fox.shape[1] == kb and w_branch.shape[1] == 4 * kb
    a_spec = lambda cb: pl.BlockSpec((tm, kb), lambda i, j: (i, cb))
    w_spec = lambda rb: pl.BlockSpec((None, kb, tn), lambda i, j: (layer, rb, j))
    g_spec = lambda br: pl.BlockSpec((tm, tn), lambda i, j: (i, br * nj + j))
    return pl.pallas_call(
        functools.partial(_merge_kernel, chunks=chunks),
        grid=(m // tm, nj),
        in_specs=[a_spec(0), a_spec(0), a_spec(1), a_spec(0)] + [w_spec(rb) for rb in range(4)]
                 + [g_spec(br) for br in range(3)],
        out_specs=pl.BlockSpec((tm, tn), lambda i, j: (i, j)),
        out_shape=jax.ShapeDtypeStruct((m, d), BF16),
        compiler_params=_params("parallel", "arbitrary"),
        name="gated_merge",
    )(o_mla, o_swa, o_swa, o_fox, w_branch, w_branch, w_branch, w_branch, gates, gates, gates)


def _ffn_up_kernel(h_ref, r_ref, wg_ref, wu_ref, src_ref, o_ref, dst_ref, *, chunks):
    dst_ref[...] = src_ref[...].astype(dst_ref.dtype)
    for rs in _row_chunks(h_ref.shape[0], chunks):
        r = r_ref[rs, 0:1]
        gate = _dot(h_ref[rs, :], wg_ref[...]) * r
        up = _dot(h_ref[rs, :], wu_ref[...]) * r
        o_ref[rs, :] = (gate * _sigmoid(gate) * up).astype(o_ref.dtype)


def ffn_up(h, r, w_gate_up, layer, side, tm=2048, tn=256, chunks=8):
    m, d = h.shape
    n = w_gate_up.shape[2] // 2
    assert n % tn == 0
    nj = n // tn
    grid = (m // tm, nj)
    side_in, side_out, side_shape = _side_cast_specs(side, grid)
    return pl.pallas_call(
        functools.partial(_ffn_up_kernel, chunks=chunks),
        grid=grid,
        in_specs=[pl.BlockSpec((tm, d), lambda i, j: (i, 0)),
                  pl.BlockSpec((tm, LANES), lambda i, j: (i, 0)),
                  pl.BlockSpec((None, d, tn), lambda i, j: (layer, 0, j)),
                  pl.BlockSpec((None, d, tn), lambda i, j: (layer, 0, nj + j)),
                  side_in],
        out_specs=[pl.BlockSpec((tm, tn), lambda i, j: (i, j)), side_out],
        out_shape=[jax.ShapeDtypeStruct((m, n), BF16), side_shape],
        compiler_params=_params("arbitrary", "arbitrary"),
        name="ffn_up",
    )(h, r, w_gate_up, w_gate_up, side.src)


def _regroup_kernel(main_ref, next_ref, o_ref, *, shift):
    x = jnp.concatenate([main_ref[shift:, :], next_ref[:shift, :]], axis=0)
    o_ref[...] = jnp.transpose(x).astype(o_ref.dtype)


def regroup_cast(w_t, row0, width, tile=512, tk=2048):
    depth, _, k = w_t.shape
    shift = row0 % LANES
    base = row0 - shift
    assert shift % 8 == 0 and shift and base % tile == 0 and width % tile == 0 and k % tk == 0
    return pl.pallas_call(
        functools.partial(_regroup_kernel, shift=shift),
        grid=(depth, k // tk, width // tile),
        in_specs=[pl.BlockSpec((None, tile, tk), lambda l, kb, c: (l, base // tile + c, kb)),
                  pl.BlockSpec((None, LANES, tk), lambda l, kb, c: (l, (base + (c + 1) * tile) // LANES, kb))],
        out_specs=pl.BlockSpec((None, tk, tile), lambda l, kb, c: (l, kb, c)),
        out_shape=jax.ShapeDtypeStruct((depth, k, width), BF16),
        compiler_params=_params("parallel", "parallel", "arbitrary"),
        name="regroup_cast",
    )(w_t, w_t)


def _latent_weight_kernel(main_ref, kr_ref, z_ref, o_ref):
    half = MLA_ROPE // 2
    kr = kr_ref[...]
    pad = jnp.zeros((LANES - FOX_HEADS, kr.shape[1]), kr.dtype)
    tail = jnp.concatenate([kr, kr[half:, :], kr[:half, :], z_ref[...], pad], axis=0)
    n_main = main_ref.shape[0]
    o_ref[:, :n_main] = jnp.transpose(main_ref[...]).astype(o_ref.dtype)
    o_ref[:, n_main:] = jnp.transpose(tail).astype(o_ref.dtype)


def latent_weights(w_t, kr_row, z_row, tk=1024):
    depth, _, k = w_t.shape
    n_main = kr_row
    assert n_main % LANES == 0 and kr_row % MLA_ROPE == 0 and z_row % FOX_HEADS == 0 and FOX_HEADS % 8 == 0
    return pl.pallas_call(
        _latent_weight_kernel,
        grid=(depth, k // tk),
        in_specs=[pl.BlockSpec((None, n_main, tk), lambda l, kb: (l, 0, kb)),
                  pl.BlockSpec((None, MLA_ROPE, tk), lambda l, kb: (l, kr_row // MLA_ROPE, kb)),
                  pl.BlockSpec((None, FOX_HEADS, tk), lambda l, kb: (l, z_row // FOX_HEADS, kb))],
        out_specs=pl.BlockSpec((None, tk, LAT_WIDTH), lambda l, kb: (l, kb, 0)),
        out_shape=jax.ShapeDtypeStruct((depth, k, LAT_WIDTH), BF16),
        compiler_params=_params("parallel", "parallel"),
        name="latent_weights",
    )(w_t, w_t, w_t)


def _input_proj_weights(w_in):
    d = w_in.shape[1]
    w_t = jnp.swapaxes(w_in, 1, 2)
    o = 0
    cuts = {}
    for name, width in (("cq", MLA_Q_LORA), ("ckv", MLA_KV_LORA), ("kr", MLA_ROPE), ("swa", SWA_WIDTH),
                        ("fox", FOX_WIDTH), ("z", FOX_HEADS), ("gate", 3 * d)):
        cuts[name] = (o, o + width)
        o += width
    assert cuts["cq"][0] == 0 and cuts["kr"][0] == MLA_Q_LORA + MLA_KV_LORA
    w_lat = latent_weights(w_t, cuts["kr"][0], cuts["z"][0])
    w_gate = regroup_cast(w_t, cuts["gate"][0], 3 * d)
    w_fox = regroup_cast(w_t, cuts["fox"][0], FOX_WIDTH)
    w_swa = regroup_cast(w_t, cuts["swa"][0], SWA_WIDTH)
    return w_gate, w_fox, w_swa, w_lat


def _q_up_weights(w_uq):
    depth = w_uq.shape[0]
    half = MLA_ROPE // 2
    uq = w_uq.reshape(depth, MLA_Q_LORA, MLA_HEADS, MLA_QK)
    rope = uq[..., MLA_NOPE:]
    rope_swapped = jnp.concatenate([rope[..., half:], rope[..., :half]], axis=-1)
    w_q = jnp.concatenate([uq[..., :MLA_NOPE], rope, rope_swapped], axis=-1)
    return w_q.reshape(depth, MLA_Q_LORA, MLA_HEADS * 2 * LANES).astype(BF16)


def _rope_consts():
    half = MLA_ROPE // 2
    inv_freq = ROPE_THETA ** (-jnp.arange(half, dtype=F32) / half)
    zeros = jnp.zeros((LANES - MLA_ROPE,), F32)
    ones = jnp.ones((half,), F32)
    rows = jnp.stack([jnp.concatenate([inv_freq, inv_freq, zeros]),
                      jnp.concatenate([ones, ones, zeros]),
                      jnp.concatenate([-ones, ones, zeros])])
    return jnp.concatenate([rows, jnp.zeros((5, LANES), F32)], axis=0)


def kernel(x, positions, g_mix_pre, g_mix_post, g_ffn_pre, g_ffn_post, w_in, g_q_lora, w_uq, g_kv_lora, w_ukv,
           b_forget, swa_sinks, w_branch, w_out, w_gate_up, w_down):
    batch, seq, d = x.shape
    depth = w_in.shape[0]
    m = batch * seq
    xs = x.reshape(m, d)
    pos_b = jnp.broadcast_to(positions.reshape(m, 1), (m, LANES))
    pos_row = positions.reshape(m // WINDOW, 1, WINDOW)
    slopes = 2.0 ** (-8.0 * jnp.arange(1, SWA_HEADS + 1, dtype=F32) / SWA_HEADS)
    cos_t, sin_t = rope_tables(pos_b, _rope_consts())
    vec = lambda g: g.reshape(1, -1)

    w_gate, w_fox, w_swa, w_lat = _input_proj_weights(w_in)
    w_q = _q_up_weights(w_uq)
    w_kv = w_ukv.astype(BF16)
    b_rows = jnp.zeros((depth, 1, LANES), F32).at[:, 0, :FOX_HEADS].set(b_forget)
    q_scale = HEAD_DIM ** -0.5 * LOG2_E
    qkv_scale = lambda q_width, width: jnp.concatenate(
        [jnp.full((1, q_width), q_scale, F32), jnp.ones((1, width - q_width), F32)], axis=1)
    fox_scale = qkv_scale(FOX_HEADS * HEAD_DIM, FOX_WIDTH)
    swa_scale = qkv_scale(SWA_HEADS * HEAD_DIM, SWA_WIDTH)

    h, r = prenorm(xs, vec(g_mix_pre[0]))
    for l in range(depth):
        gates, w_gu = matmul(h, w_gate, l, 3 * d, BF16, 1024, 1024, row_scale=r, act="sigmoid", chunks=1,
                             side=SideCast(w_gate_up, l), name="gate_proj")
        qkv_fox, w_br = matmul(h, w_fox, l, FOX_WIDTH, BF16, 1024, 1024, row_scale=r, col_scale=fox_scale,
                               chunks=1, side=SideCast(w_branch, l), name="fox_proj")
        qkv_swa = matmul(h, w_swa, l, SWA_WIDTH, BF16, 1024, SWA_WIDTH // 2, row_scale=r, col_scale=swa_scale,
                         chunks=1, name="swa_proj")
        cqn, ckvn, k_rot, z_slab, w_o = latent_proj(h, r, w_lat, l, vec(g_q_lora[l]), vec(g_kv_lora[l]),
                                                    cos_t, sin_t, SideCast(w_out, l))
        q_mla = q_up_proj(cqn, w_q, l, cos_t, sin_t)
        kv_mla = matmul(ckvn, w_kv, l, w_kv.shape[2], BF16, 1024, 1024, name="kv_up_proj")
        f_t = fox_decay(z_slab, b_rows[l], batch, seq)
        o_mla = mla_attention(q_mla, kv_mla, k_rot, batch, seq)
        o_swa = swa_attention(qkv_swa, pos_b, pos_row, slopes * LOG2_E, swa_sinks[l].astype(F32) * LOG2_E,
                              batch, seq)
        o_fox = fox_attention(qkv_fox, f_t, batch, seq)
        merged = gated_merge(o_mla, o_swa, o_fox, w_br, 0, gates)
        xs, h2, r2 = proj_norm(merged, w_o, 0, xs, vec(g_mix_post[l]), vec(g_ffn_pre[l]), 512, 1024)
        act, w_dn = ffn_up(h2, r2, w_gu, 0, SideCast(w_down, l))
        g_next = vec(g_mix_pre[l + 1]) if l + 1 < depth else None
        xs, h, r = proj_norm(act, w_dn, 0, xs, vec(g_ffn_post[l]), g_next, 512, 256)
    return xs.reshape(batch, seq, d)
```

```python
import functools
import math
from typing import Any, NamedTuple

import jax
import jax.numpy as jnp
from jax import lax
from jax.experimental import pallas as pl
from jax.experimental.pallas import tpu as pltpu

F32 = jnp.float32
BF16 = jnp.bfloat16

VMEM_LIMIT_BYTES = 56 * 1024 * 1024
LANES = 128

HEAD_DIM = 128
MLA_HEADS = 8
MLA_Q_LORA = 1024
MLA_KV_LORA = 512
MLA_NOPE = 128
MLA_ROPE = 64
MLA_QK = MLA_NOPE + MLA_ROPE
SWA_HEADS = 16
SWA_KV_HEADS = 2
SWA_GROUP = SWA_HEADS // SWA_KV_HEADS
WINDOW = 128
FOX_HEADS = 8
ROPE_THETA = 10000.0
RMS_EPS = 1e-6
LOG2_E = math.log2(math.e)
Q_TILE = 512
HEADS_PER_STEP = 2

SWA_WIDTH = (SWA_HEADS + 2 * SWA_KV_HEADS) * HEAD_DIM
FOX_WIDTH = 3 * FOX_HEADS * HEAD_DIM
LAT_WIDTH = MLA_Q_LORA + MLA_KV_LORA + 2 * LANES


def _params(*sem):
    return pltpu.CompilerParams(dimension_semantics=sem, vmem_limit_bytes=VMEM_LIMIT_BYTES)


def _inv_rms(x):
    return lax.rsqrt(jnp.mean(x * x, axis=-1, keepdims=True) + RMS_EPS)


def _dot(a, b):
    return jnp.dot(a, b, preferred_element_type=F32)


def _dot_nt(a, b):
    return lax.dot_general(a, b, (((1,), (1,)), ((), ())), preferred_element_type=F32)


def _sigmoid(x):
    return 0.5 * jnp.tanh(0.5 * x) + 0.5


def _row_chunks(rows, chunks):
    step = rows // chunks
    return [slice(c * step, (c + 1) * step) for c in range(chunks)]


def _rope_table_kernel(pos_ref, c_ref, cos_ref, sin_ref):
    ang = pos_ref[...].astype(F32) * c_ref[0:1, :]
    cos_ref[...] = jnp.cos(ang) * c_ref[1:2, :]
    sin_ref[...] = jnp.sin(ang) * c_ref[2:3, :]


def rope_tables(pos_b, consts, tm=2048):
    m = pos_b.shape[0]
    spec = pl.BlockSpec((tm, LANES), lambda i: (i, 0))
    return pl.pallas_call(
        _rope_table_kernel,
        grid=(m // tm,),
        in_specs=[spec, pl.BlockSpec((8, LANES), lambda i: (0, 0))],
        out_specs=[spec, spec],
        out_shape=[jax.ShapeDtypeStruct((m, LANES), F32)] * 2,
        compiler_params=_params("parallel"),
        name="rope_tables",
    )(pos_b, consts)


def _prenorm_kernel(x_ref, g_ref, h_ref, r_ref):
    x = x_ref[...]
    h_ref[...] = (x * g_ref[...]).astype(h_ref.dtype)
    r_ref[...] = jnp.broadcast_to(_inv_rms(x), r_ref.shape)


def prenorm(x, gain, tm=256):
    m, d = x.shape
    return pl.pallas_call(
        _prenorm_kernel,
        grid=(m // tm,),
        in_specs=[pl.BlockSpec((tm, d), lambda i: (i, 0)), pl.BlockSpec((1, d), lambda i: (0, 0))],
        out_specs=[pl.BlockSpec((tm, d), lambda i: (i, 0)), pl.BlockSpec((tm, LANES), lambda i: (i, 0))],
        out_shape=[jax.ShapeDtypeStruct((m, d), BF16), jax.ShapeDtypeStruct((m, LANES), F32)],
        compiler_params=_params("parallel"),
        name="prenorm",
    )(x, gain)


class SideCast(NamedTuple):
    src: Any
    layer: int


BF16_SUBLANES = 16


def _side_cast_specs(side, grid):
    _, rows, cols = side.src.shape
    steps = math.prod(grid)
    n_blocks = max(nb for nb in range(1, steps + 1) if rows % (nb * BF16_SUBLANES) == 0)
    br = rows // n_blocks

    def row_block(*ids):
        step = 0
        for extent, idx in zip(grid, ids):
            step = step * extent + idx
        return jnp.minimum(step, n_blocks - 1)

    in_spec = pl.BlockSpec((None, br, cols), lambda *ids: (side.layer, row_block(*ids), 0))
    out_spec = pl.BlockSpec((None, br, cols), lambda *ids: (0, row_block(*ids), 0))
    return in_spec, out_spec, jax.ShapeDtypeStruct((1, rows, cols), BF16)


def _matmul_kernel(*refs, act, row_scaled, col_scaled, side, chunks):
    refs = list(refs)
    a_ref = refs.pop(0)
    r_ref = refs.pop(0) if row_scaled else None
    c_ref = refs.pop(0) if col_scaled else None
    w_ref = refs.pop(0)
    if side:
        src_ref, o_ref, dst_ref = refs
        dst_ref[...] = src_ref[...].astype(dst_ref.dtype)
    else:
        (o_ref,) = refs
    for rs in _row_chunks(a_ref.shape[0], chunks):
        acc = _dot(a_ref[rs, :], w_ref[...])
        if row_scaled:
            acc = acc * r_ref[rs, 0:1]
        if col_scaled:
            acc = acc * c_ref[...]
        if act == "sigmoid":
            acc = _sigmoid(acc)
        o_ref[rs, :] = acc.astype(o_ref.dtype)


def matmul(a, w, layer, n, out_dtype, tm, tn, row_scale=None, col_scale=None, act=None, side=None, chunks=1,
           name="matmul"):
    m, k = a.shape
    assert m % tm == 0 and n % tn == 0 and w.shape[1:] == (k, n)
    grid = (m // tm, n // tn)
    in_specs = [pl.BlockSpec((tm, k), lambda i, j: (i, 0))]
    args = [a]
    if row_scale is not None:
        in_specs.append(pl.BlockSpec((tm, LANES), lambda i, j: (i, 0)))
        args.append(row_scale)
    if col_scale is not None:
        in_specs.append(pl.BlockSpec((1, tn), lambda i, j: (0, j)))
        args.append(col_scale)
    in_specs.append(pl.BlockSpec((None, k, tn), lambda i, j: (layer, 0, j)))
    args.append(w)
    out_specs = [pl.BlockSpec((tm, tn), lambda i, j: (i, j))]
    out_shape = [jax.ShapeDtypeStruct((m, n), out_dtype)]
    if side is not None:
        side_in, side_out, side_shape = _side_cast_specs(side, grid)
        in_specs.append(side_in)
        args.append(side.src)
        out_specs.append(side_out)
        out_shape.append(side_shape)
    res = pl.pallas_call(
        functools.partial(_matmul_kernel, act=act, row_scaled=row_scale is not None,
                          col_scaled=col_scale is not None, side=side is not None, chunks=chunks),
        grid=grid,
        in_specs=in_specs,
        out_specs=out_specs,
        out_shape=out_shape,
        compiler_params=_params("arbitrary", "arbitrary"),
        name=name,
    )(*args)
    return res if side is not None else res[0]


def _proj_norm_kernel(*refs, n_i, n_j, d, emit_next):
    if emit_next:
        a_ref, w_ref, x_ref, gp_ref, gn_ref, xo_ref, h_ref, r_ref, y_scr, acc_scr, done_scr, acc2_scr = refs
    else:
        a_ref, w_ref, x_ref, gp_ref, xo_ref, y_scr, acc_scr, done_scr = refs
    i = pl.program_id(0)
    j = pl.program_id(1)

    @pl.when((i == 0) & (j == 0))
    def _():
        y_scr[...] = jnp.zeros_like(y_scr)
        acc_scr[...] = jnp.zeros_like(acc_scr)
        done_scr[...] = jnp.zeros_like(done_scr)
        if emit_next:
            acc2_scr[...] = jnp.zeros_like(acc2_scr)

    first = j == 0
    y_prev = y_scr[j]
    y_new = _dot(a_ref[...], w_ref[...])
    y_scr[j] = y_new
    part = jnp.sum(y_new * y_new, axis=-1, keepdims=True)
    acc_old = acc_scr[...]
    done = jnp.where(first, acc_old, done_scr[...])
    done_scr[...] = done
    acc_scr[...] = jnp.where(first, part, acc_old + part)

    x_new = x_ref[...] + y_prev * lax.rsqrt(done * (1.0 / d) + RMS_EPS) * gp_ref[...]
    xo_ref[...] = x_new
    if emit_next:
        h_ref[...] = (x_new * gn_ref[...]).astype(h_ref.dtype)
        part2 = jnp.sum(x_new * x_new, axis=-1, keepdims=True)
        acc2 = jnp.where(first, part2, acc2_scr[...] + part2)
        acc2_scr[...] = acc2
        r_ref[...] = jnp.broadcast_to(lax.rsqrt(acc2 * (1.0 / d) + RMS_EPS), r_ref.shape)


def proj_norm(a, w, layer, x, g_post, g_next, tm, tn, single_buffer_a=False):
    m, k = a.shape
    d = w.shape[2]
    n_i, n_j = m // tm, d // tn
    emit_next = g_next is not None
    prev = lambda i: jnp.maximum(i - 1, 0)
    col = pl.BlockSpec((tm, tn), lambda i, j: (prev(i), j))
    col_out = pl.BlockSpec((tm, tn), lambda i, j: (prev(i), jnp.where(i == 0, 0, j)))
    vec = pl.BlockSpec((1, tn), lambda i, j: (0, j))
    a_mode = pl.Buffered(1) if single_buffer_a else None
    in_specs = [pl.BlockSpec((tm, k), lambda i, j: (jnp.minimum(i, n_i - 1), 0), pipeline_mode=a_mode),
                pl.BlockSpec((None, k, tn), lambda i, j: (layer, 0, j)),
                col, vec]
    args = [a, w, x, g_post]
    out_specs = [col_out]
    out_shape = [jax.ShapeDtypeStruct((m, d), F32)]
    scratch = [pltpu.VMEM((n_j, tm, tn), F32), pltpu.VMEM((tm, 1), F32), pltpu.VMEM((tm, 1), F32)]
    if emit_next:
        in_specs.append(vec)
        args.append(g_next)
        out_specs += [col_out, pl.BlockSpec((tm, LANES), lambda i, j: (prev(i), 0))]
        out_shape += [jax.ShapeDtypeStruct((m, d), BF16), jax.ShapeDtypeStruct((m, LANES), F32)]
        scratch.append(pltpu.VMEM((tm, 1), F32))
    res = pl.pallas_call(
        functools.partial(_proj_norm_kernel, n_i=n_i, n_j=n_j, d=d, emit_next=emit_next),
        grid=(n_i + 1, n_j),
        in_specs=in_specs,
        out_specs=out_specs,
        out_shape=out_shape,
        scratch_shapes=scratch,
        compiler_params=_params("arbitrary", "arbitrary"),
        name="proj_norm" if emit_next else "proj_norm_last",
    )(*args)
    return res if emit_next else (res[0], None, None)


def _latent_kernel(h_ref, r_ref, w_ref, gq_ref, gkv_ref, cos_ref, sin_ref, src_ref,
                   cq_ref, ckv_ref, kr_ref, z_ref, dst_ref, *, chunks):
    dst_ref[...] = src_ref[...].astype(dst_ref.dtype)
    q_end = MLA_Q_LORA
    kv_end = q_end + MLA_KV_LORA
    for rs in _row_chunks(h_ref.shape[0], chunks):
        acc = _dot(h_ref[rs, :], w_ref[...]) * r_ref[rs, 0:1]
        cq = acc[:, :q_end]
        ckv = acc[:, q_end:kv_end]
        cq_ref[rs, :] = (cq * _inv_rms(cq) * gq_ref[...]).astype(cq_ref.dtype)
        ckv_ref[rs, :] = (ckv * _inv_rms(ckv) * gkv_ref[...]).astype(ckv_ref.dtype)
        slab = acc[:, kv_end:kv_end + LANES]
        rot = slab * cos_ref[rs, :] + pltpu.roll(slab, LANES // 2, 1) * sin_ref[rs, :]
        kr_ref[rs, :] = rot.astype(kr_ref.dtype)
        z_ref[rs, :] = acc[:, kv_end + LANES:kv_end + 2 * LANES]


def latent_proj(h, r, w_lat, layer, g_q, g_kv, cos_t, sin_t, side, tm=512, chunks=4):
    m, d = h.shape
    n = LAT_WIDTH
    grid = (m // tm,)
    row = lambda width: pl.BlockSpec((tm, width), lambda i: (i, 0))
    const = lambda c: pl.BlockSpec((1, c), lambda i: (0, 0))
    side_in, side_out, side_shape = _side_cast_specs(side, grid)
    return pl.pallas_call(
        functools.partial(_latent_kernel, chunks=chunks),
        grid=grid,
        in_specs=[row(d), row(LANES), pl.BlockSpec((None, d, n), lambda i: (layer, 0, 0)),
                  const(MLA_Q_LORA), const(MLA_KV_LORA), row(LANES), row(LANES), side_in],
        out_specs=[row(MLA_Q_LORA), row(MLA_KV_LORA), row(LANES), row(LANES), side_out],
        out_shape=[jax.ShapeDtypeStruct((m, MLA_Q_LORA), BF16), jax.ShapeDtypeStruct((m, MLA_KV_LORA), BF16),
                   jax.ShapeDtypeStruct((m, LANES), BF16), jax.ShapeDtypeStruct((m, LANES), F32), side_shape],
        compiler_params=_params("arbitrary"),
        name="latent_proj",
    )(h, r, w_lat, g_q, g_kv, cos_t, sin_t, side.src)


def _q_up_kernel(a_ref, w_ref, cos_ref, sin_ref, o_ref, *, scale):
    acc = _dot(a_ref[...], w_ref[...])
    c = cos_ref[...]
    s = sin_ref[...]
    for h in range(MLA_HEADS):
        base = 2 * LANES * h
        nope = acc[:, base:base + LANES]
        slab = acc[:, base + LANES:base + 2 * LANES]
        rot = slab * c + pltpu.roll(slab, LANES // 2, 1) * s
        o_ref[:, base:base + LANES] = (nope * scale).astype(o_ref.dtype)
        o_ref[:, base + LANES:base + 2 * LANES] = (rot * scale).astype(o_ref.dtype)


def q_up_proj(cqn, w_uq, layer, cos_t, sin_t, tm=1024):
    m, k = cqn.shape
    n = w_uq.shape[2]
    return pl.pallas_call(
        functools.partial(_q_up_kernel, scale=MLA_QK ** -0.5 * LOG2_E),
        grid=(m // tm,),
        in_specs=[pl.BlockSpec((tm, k), lambda i: (i, 0)), pl.BlockSpec((None, k, n), lambda i: (layer, 0, 0)),
                  pl.BlockSpec((tm, LANES), lambda i: (i, 0)), pl.BlockSpec((tm, LANES), lambda i: (i, 0))],
        out_specs=pl.BlockSpec((tm, n), lambda i: (i, 0)),
        out_shape=jax.ShapeDtypeStruct((m, n), BF16),
        compiler_params=_params("parallel"),
        name="q_up_proj",
    )(cqn, w_uq, cos_t, sin_t)


class _Head(NamedTuple):
    q: Any
    q_cols: slice
    k: Any
    k_cols: slice
    v: Any
    v_cols: slice
    o: Any
    o_cols: slice
    fq: Any = None
    fk_row: Any = None


def _causal_blocks(heads, seq, tq, longest_first):
    row = lax.broadcasted_iota(jnp.int32, (tq, tq), 0)
    col = lax.broadcasted_iota(jnp.int32, (tq, tq), 1)
    tri = col <= row
    tiles = range(seq // tq)
    for i in (reversed(tiles) if longest_first else tiles):
        lo, hi = i * tq, (i + 1) * tq
        for hd in heads:
            qi = hd.q[lo:hi, hd.q_cols]
            s_d = _dot_nt(qi, hd.k[lo:hi, hd.k_cols])
            if hd.fk_row is not None:
                s_d = s_d - hd.fk_row[:, lo:hi]
            s_d = jnp.where(tri, s_d, -jnp.inf)
            mx = jnp.max(s_d, axis=-1, keepdims=True)
            if i > 0:
                s_o = _dot_nt(qi, hd.k[:lo, hd.k_cols])
                if hd.fk_row is not None:
                    s_o = s_o - hd.fk_row[:, :lo]
                mx = jnp.maximum(mx, jnp.max(s_o, axis=-1, keepdims=True))
            if hd.fq is not None:
                fq = hd.fq[lo:hi, 0:1]
                off = fq - (mx + fq)
            else:
                off = -mx
            p_d = jnp.exp2(s_d + off)
            den = jnp.sum(p_d, axis=-1, keepdims=True)
            acc = _dot(p_d.astype(BF16), hd.v[lo:hi, hd.v_cols])
            if i > 0:
                p_o = jnp.exp2(s_o + off)
                den = den + jnp.sum(p_o, axis=-1, keepdims=True)
                acc = acc + _dot(p_o.astype(BF16), hd.v[:lo, hd.v_cols])
            hd.o[lo:hi, hd.o_cols] = (acc / den).astype(hd.o.dtype)


def _mla_attn_kernel(q_ref, kv_ref, kr_ref, o_ref, k_scr, *, seq, tq, hps):
    heads = []
    for u in range(hps):
        base = 2 * LANES * u
        k_scr[u, :, :LANES] = kv_ref[:, base:base + LANES]
        k_scr[u, :, LANES:] = kr_ref[...]
        heads.append(_Head(q_ref, slice(base, base + 2 * LANES), k_scr.at[u], slice(None),
                           kv_ref, slice(base + LANES, base + 2 * LANES), o_ref, slice(LANES * u, LANES * (u + 1))))
    _causal_blocks(heads, seq, tq, longest_first=True)


def mla_attention(q, kv, k_rot, batch, seq, hps=HEADS_PER_STEP):
    m = q.shape[0]
    return pl.pallas_call(
        functools.partial(_mla_attn_kernel, seq=seq, tq=Q_TILE, hps=hps),
        grid=(batch, MLA_HEADS // hps),
        in_specs=[pl.BlockSpec((seq, 2 * LANES * hps), lambda b, g: (b, g)),
                  pl.BlockSpec((seq, 2 * LANES * hps), lambda b, g: (b, g)),
                  pl.BlockSpec((seq, LANES), lambda b, g: (b, 0))],
        out_specs=pl.BlockSpec((seq, LANES * hps), lambda b, g: (b, g)),
        out_shape=jax.ShapeDtypeStruct((m, MLA_HEADS * LANES), BF16),
        scratch_shapes=[pltpu.VMEM((hps, seq, 2 * LANES), BF16)],
        compiler_params=_params("parallel", "arbitrary"),
        name="mla_attention",
    )(q, kv, k_rot)


def _fox_attn_kernel(q_ref, k_ref, v_ref, ft_ref, o_ref, fq_scr, qx_scr, kx_scr, *, seq, tq, hps):
    g = pl.program_id(1)
    lane = lax.broadcasted_iota(jnp.int32, (seq, LANES), 1)
    ones = jnp.where(lane < 3, 1.0, 0.0).astype(BF16)
    heads = []
    for u in range(hps):
        fk_row = ft_ref[0, pl.ds(g * hps + u, 1), :]
        f = jnp.transpose(jnp.broadcast_to(fk_row, (LANES, seq)))
        fq_scr[u] = f
        hi = f.astype(BF16).astype(F32)
        mid = (f - hi).astype(BF16).astype(F32)
        lo = f - hi - mid
        pieces = jnp.where(lane == 0, hi, jnp.where(lane == 1, mid, jnp.where(lane == 2, lo, 0.0)))
        cols = slice(LANES * u, LANES * (u + 1))
        qx_scr[u, :, :LANES] = q_ref[:, cols]
        qx_scr[u, :, LANES:] = ones
        kx_scr[u, :, :LANES] = k_ref[:, cols]
        kx_scr[u, :, LANES:] = (-pieces).astype(BF16)
        heads.append(_Head(qx_scr.at[u], slice(None), kx_scr.at[u], slice(None), v_ref, cols, o_ref, cols,
                           fq_scr.at[u], None))
    _causal_blocks(heads, seq, tq, longest_first=False)


def fox_attention(qkv, f_t, batch, seq, hps=HEADS_PER_STEP):
    m = qkv.shape[0]
    nh = FOX_HEADS
    ng = nh // hps
    blk = lambda part: pl.BlockSpec((seq, LANES * hps), lambda b, g: (b, part * ng + g))
    return pl.pallas_call(
        functools.partial(_fox_attn_kernel, seq=seq, tq=Q_TILE, hps=hps),
        grid=(batch, ng),
        in_specs=[blk(0), blk(1), blk(2), pl.BlockSpec((1, nh, seq), lambda b, g: (b, 0, 0))],
        out_specs=pl.BlockSpec((seq, LANES * hps), lambda b, g: (b, g)),
        out_shape=jax.ShapeDtypeStruct((m, nh * LANES), BF16),
        scratch_shapes=[pltpu.VMEM((hps, seq, LANES), F32), pltpu.VMEM((hps, seq, 2 * LANES), BF16),
                        pltpu.VMEM((hps, seq, 2 * LANES), BF16)],
        compiler_params=_params("parallel", "arbitrary"),
        name="fox_attention",
    )(qkv, qkv, qkv, f_t)


def _fox_decay_kernel(z_ref, b_ref, ft_ref, *, seq):
    x = z_ref[...] + b_ref[...]
    log_f = jnp.minimum(x, 0.0) - jnp.log1p(jnp.exp(-jnp.abs(x)))
    t = jnp.transpose(log_f)[0:FOX_HEADS, :]
    lane = lax.broadcasted_iota(jnp.int32, t.shape, 1)
    shift = 1
    while shift < seq:
        t = t + jnp.where(lane >= shift, pltpu.roll(t, shift, 1), 0.0)
        shift *= 2
    ft_ref[0] = t * LOG2_E


def fox_decay(z_slab, b_row, batch, seq):
    return pl.pallas_call(
        functools.partial(_fox_decay_kernel, seq=seq),
        grid=(batch,),
        in_specs=[pl.BlockSpec((seq, LANES), lambda b: (b, 0)), pl.BlockSpec((1, LANES), lambda b: (0, 0))],
        out_specs=pl.BlockSpec((1, FOX_HEADS, seq), lambda b: (b, 0, 0)),
        out_shape=jax.ShapeDtypeStruct((batch, FOX_HEADS, seq), F32),
        compiler_params=_params("parallel"),
        name="fox_decay",
    )(z_slab, b_row)


def _swa_kernel(slope_ref, sink_ref, q_ref, kp_ref, kc_ref, vp_ref, vc_ref, qpos_ref, pp_ref, pc_ref, o_ref):
    n = pl.program_id(1)
    w = WINDOW
    i = lax.broadcasted_iota(jnp.int32, (w, w), 0)
    j = lax.broadcasted_iota(jnp.int32, (w, w), 1)
    valid = jnp.concatenate([(j > i) & (n > 0), j <= i], axis=1)
    qpos = qpos_ref[:, 0:1]
    dist = jnp.concatenate([qpos - pp_ref[0], qpos - pc_ref[0]], axis=1).astype(F32)
    neg_dist = jnp.where(valid, -dist, -jnp.inf)
    for kh in range(SWA_KV_HEADS):
        ks = slice(kh * HEAD_DIM, (kh + 1) * HEAD_DIM)
        k_win = jnp.concatenate([kp_ref[:, ks], kc_ref[:, ks]], axis=0)
        v_win = jnp.concatenate([vp_ref[:, ks], vc_ref[:, ks]], axis=0)
        for g in range(SWA_GROUP):
            hd = kh * SWA_GROUP + g
            hs = slice(hd * HEAD_DIM, (hd + 1) * HEAD_DIM)
            s = _dot_nt(q_ref[:, hs], k_win) + slope_ref[hd] * neg_dist
            sink = sink_ref[hd]
            mx = jnp.maximum(jnp.max(s, axis=-1, keepdims=True), sink)
            e = jnp.exp2(s - mx)
            den = jnp.sum(e, axis=-1, keepdims=True) + jnp.exp2(sink - mx)
            o_ref[:, hs] = (_dot(e.astype(BF16), v_win) / den).astype(o_ref.dtype)


def swa_attention(qkv, pos_b, pos_row, slopes, sinks, batch, seq):
    m = qkv.shape[0]
    w = WINDOW
    nb = seq // w
    qw = SWA_HEADS * HEAD_DIM
    kvw = SWA_KV_HEADS * HEAD_DIM
    cur = lambda b, n: b * nb + n
    prev = lambda b, n: b * nb + jnp.maximum(n - 1, 0)
    smem = pl.BlockSpec(memory_space=pltpu.SMEM)
    return pl.pallas_call(
        _swa_kernel,
        grid=(batch, nb),
        in_specs=[smem, smem,
                  pl.BlockSpec((w, qw), lambda b, n: (cur(b, n), 0)),
                  pl.BlockSpec((w, kvw), lambda b, n: (prev(b, n), qw // kvw)),
                  pl.BlockSpec((w, kvw), lambda b, n: (cur(b, n), qw // kvw)),
                  pl.BlockSpec((w, kvw), lambda b, n: (prev(b, n), qw // kvw + 1)),
                  pl.BlockSpec((w, kvw), lambda b, n: (cur(b, n), qw // kvw + 1)),
                  pl.BlockSpec((w, LANES), lambda b, n: (cur(b, n), 0)),
                  pl.BlockSpec((1, 1, w), lambda b, n: (prev(b, n), 0, 0)),
                  pl.BlockSpec((1, 1, w), lambda b, n: (cur(b, n), 0, 0))],
        out_specs=pl.BlockSpec((w, qw), lambda b, n: (cur(b, n), 0)),
        out_shape=jax.ShapeDtypeStruct((m, qw), BF16),
        compiler_params=_params("parallel", "arbitrary"),
        name="swa_attention",
    )(slopes, sinks, qkv, qkv, qkv, qkv, qkv, pos_b, pos_row, pos_row)


def _merge_kernel(mla_ref, swa_lo_ref, swa_hi_ref, fox_ref, w0_ref, w1_ref, w2_ref, w3_ref,
                  g0_ref, g1_ref, g2_ref, o_ref, *, chunks):
    for rs in _row_chunks(o_ref.shape[0], chunks):
        acc = g0_ref[rs, :].astype(F32) * _dot(mla_ref[rs, :], w0_ref[...])
        swa = _dot(swa_lo_ref[rs, :], w1_ref[...]) + _dot(swa_hi_ref[rs, :], w2_ref[...])
        acc = acc + g1_ref[rs, :].astype(F32) * swa
        acc = acc + g2_ref[rs, :].astype(F32) * _dot(fox_ref[rs, :], w3_ref[...])
        o_ref[rs, :] = acc.astype(o_ref.dtype)


def gated_merge(o_mla, o_swa, o_fox, w_branch, layer, gates, tm=1024, tn=512, chunks=1):
    m, kb = o_mla.shape
    d = w_branch.shape[2]
    nj = d // tn
    assert o_swa.shape[1] == 2 * kb and o_fox.shape[1] == kb and w_branch.shape[1] == 4 * kb
    a_spec = lambda cb: pl.BlockSpec((tm, kb), lambda i, j: (i, cb))
    w_spec = lambda rb: pl.BlockSpec((None, kb, tn), lambda i, j: (layer, rb, j))
    g_spec = lambda br: pl.BlockSpec((tm, tn), lambda i, j: (i, br * nj + j))
    return pl.pallas_call(
        functools.partial(_merge_kernel, chunks=chunks),
        grid=(m // tm, nj),
        in_specs=[a_spec(0), a_spec(0), a_spec(1), a_spec(0)] + [w_spec(rb) for rb in range(4)]
                 + [g_spec(br) for br in range(3)],
        out_specs=pl.BlockSpec((tm, tn), lambda i, j: (i, j)),
        out_shape=jax.ShapeDtypeStruct((m, d), BF16),
        compiler_params=_params("parallel", "arbitrary"),
        name="gated_merge",
    )(o_mla, o_swa, o_swa, o_fox, w_branch, w_branch, w_branch, w_branch, gates, gates, gates)


def _ffn_up_kernel(h_ref, r_ref, wg_ref, wu_ref, src_ref, o_ref, dst_ref, *, chunks):
    dst_ref[...] = src_ref[...].astype(dst_ref.dtype)
    for rs in _row_chunks(h_ref.shape[0], chunks):
        r = r_ref[rs, 0:1]
        gate = _dot(h_ref[rs, :], wg_ref[...]) * r
        up = _dot(h_ref[rs, :], wu_ref[...]) * r
        o_ref[rs, :] = (gate * _sigmoid(gate) * up).astype(o_ref.dtype)


def ffn_up(h, r, w_gate_up, layer, side, tm=2048, tn=256, chunks=8):
    m, d = h.shape
    n = w_gate_up.shape[2] // 2
    assert n % tn == 0
    nj = n // tn
    grid = (m // tm, nj)
    side_in, side_out, side_shape = _side_cast_specs(side, grid)
    return pl.pallas_call(
        functools.partial(_ffn_up_kernel, chunks=chunks),
        grid=grid,
        in_specs=[pl.BlockSpec((tm, d), lambda i, j: (i, 0)),
                  pl.BlockSpec((tm, LANES), lambda i, j: (i, 0)),
                  pl.BlockSpec((None, d, tn), lambda i, j: (layer, 0, j)),
                  pl.BlockSpec((None, d, tn), lambda i, j: (layer, 0, nj + j)),
                  side_in],
        out_specs=[pl.BlockSpec((tm, tn), lambda i, j: (i, j)), side_out],
        out_shape=[jax.ShapeDtypeStruct((m, n), BF16), side_shape],
        compiler_params=_params("arbitrary", "arbitrary"),
        name="ffn_up",
    )(h, r, w_gate_up, w_gate_up, side.src)


def _regroup_kernel(main_ref, next_ref, o_ref, *, shift):
    x = jnp.concatenate([main_ref[shift:, :], next_ref[:shift, :]], axis=0)
    o_ref[...] = jnp.transpose(x).astype(o_ref.dtype)


def regroup_cast(w_t, row0, width, tile=512, tk=2048):
    depth, _, k = w_t.shape
    shift = row0 % LANES
    base = row0 - shift
    assert shift % 8 == 0 and shift and base % tile == 0 and width % tile == 0 and k % tk == 0
    return pl.pallas_call(
        functools.partial(_regroup_kernel, shift=shift),
        grid=(depth, k // tk, width // tile),
        in_specs=[pl.BlockSpec((None, tile, tk), lambda l, kb, c: (l, base // tile + c, kb)),
                  pl.BlockSpec((None, LANES, tk), lambda l, kb, c: (l, (base + (c + 1) * tile) // LANES, kb))],
        out_specs=pl.BlockSpec((None, tk, tile), lambda l, kb, c: (l, kb, c)),
        out_shape=jax.ShapeDtypeStruct((depth, k, width), BF16),
        compiler_params=_params("parallel", "parallel", "arbitrary"),
        name="regroup_cast",
    )(w_t, w_t)


def _latent_weight_kernel(main_ref, kr_ref, z_ref, o_ref):
    half = MLA_ROPE // 2
    kr = kr_ref[...]
    pad = jnp.zeros((LANES - FOX_HEADS, kr.shape[1]), kr.dtype)
    tail = jnp.concatenate([kr, kr[half:, :], kr[:half, :], z_ref[...], pad], axis=0)
    n_main = main_ref.shape[0]
    o_ref[:, :n_main] = jnp.transpose(main_ref[...]).astype(o_ref.dtype)
    o_ref[:, n_main:] = jnp.transpose(tail).astype(o_ref.dtype)


def latent_weights(w_t, kr_row, z_row, tk=1024):
    depth, _, k = w_t.shape
    n_main = kr_row
    assert n_main % LANES == 0 and kr_row % MLA_ROPE == 0 and z_row % FOX_HEADS == 0 and FOX_HEADS % 8 == 0
    return pl.pallas_call(
        _latent_weight_kernel,
        grid=(depth, k // tk),
        in_specs=[pl.BlockSpec((None, n_main, tk), lambda l, kb: (l, 0, kb)),
                  pl.BlockSpec((None, MLA_ROPE, tk), lambda l, kb: (l, kr_row // MLA_ROPE, kb)),
                  pl.BlockSpec((None, FOX_HEADS, tk), lambda l, kb: (l, z_row // FOX_HEADS, kb))],
        out_specs=pl.BlockSpec((None, tk, LAT_WIDTH), lambda l, kb: (l, kb, 0)),
        out_shape=jax.ShapeDtypeStruct((depth, k, LAT_WIDTH), BF16),
        compiler_params=_params("parallel", "parallel"),
        name="latent_weights",
    )(w_t, w_t, w_t)


def _input_proj_weights(w_in):
    d = w_in.shape[1]
    w_t = jnp.swapaxes(w_in, 1, 2)
    o = 0
    cuts = {}
    for name, width in (("cq", MLA_Q_LORA), ("ckv", MLA_KV_LORA), ("kr", MLA_ROPE), ("swa", SWA_WIDTH),
                        ("fox", FOX_WIDTH), ("z", FOX_HEADS), ("gate", 3 * d)):
        cuts[name] = (o, o + width)
        o += width
    assert cuts["cq"][0] == 0 and cuts["kr"][0] == MLA_Q_LORA + MLA_KV_LORA
    w_lat = latent_weights(w_t, cuts["kr"][0], cuts["z"][0])
    w_gate = regroup_cast(w_t, cuts["gate"][0], 3 * d)
    w_fox = regroup_cast(w_t, cuts["fox"][0], FOX_WIDTH)
    w_swa = regroup_cast(w_t, cuts["swa"][0], SWA_WIDTH)
    return w_gate, w_fox, w_swa, w_lat


def _q_up_weights(w_uq):
    depth = w_uq.shape[0]
    half = MLA_ROPE // 2
    uq = w_uq.reshape(depth, MLA_Q_LORA, MLA_HEADS, MLA_QK)
    rope = uq[..., MLA_NOPE:]
    rope_swapped = jnp.concatenate([rope[..., half:], rope[..., :half]], axis=-1)
    w_q = jnp.concatenate([uq[..., :MLA_NOPE], rope, rope_swapped], axis=-1)
    return w_q.reshape(depth, MLA_Q_LORA, MLA_HEADS * 2 * LANES).astype(BF16)


def _rope_consts():
    half = MLA_ROPE // 2
    inv_freq = ROPE_THETA ** (-jnp.arange(half, dtype=F32) / half)
    zeros = jnp.zeros((LANES - MLA_ROPE,), F32)
    ones = jnp.ones((half,), F32)
    rows = jnp.stack([jnp.concatenate([inv_freq, inv_freq, zeros]),
                      jnp.concatenate([ones, ones, zeros]),
                      jnp.concatenate([-ones, ones, zeros])])
    return jnp.concatenate([rows, jnp.zeros((5, LANES), F32)], axis=0)


def kernel(x, positions, g_mix_pre, g_mix_post, g_ffn_pre, g_ffn_post, w_in, g_q_lora, w_uq, g_kv_lora, w_ukv,
           b_forget, swa_sinks, w_branch, w_out, w_gate_up, w_down):
    batch, seq, d = x.shape
    depth = w_in.shape[0]
    m = batch * seq
    xs = x.reshape(m, d)
    pos_b = jnp.broadcast_to(positions.reshape(m, 1), (m, LANES))
    pos_row = positions.reshape(m // WINDOW, 1, WINDOW)
    slopes = 2.0 ** (-8.0 * jnp.arange(1, SWA_HEADS + 1, dtype=F32) / SWA_HEADS)
    cos_t, sin_t = rope_tables(pos_b, _rope_consts())
    vec = lambda g: g.reshape(1, -1)

    w_gate, w_fox, w_swa, w_lat = _input_proj_weights(w_in)
    w_q = _q_up_weights(w_uq)
    w_kv = w_ukv.astype(BF16)
    b_rows = jnp.zeros((depth, 1, LANES), F32).at[:, 0, :FOX_HEADS].set(b_forget)
    q_scale = HEAD_DIM ** -0.5 * LOG2_E
    qkv_scale = lambda q_width, width: jnp.concatenate(
        [jnp.full((1, q_width), q_scale, F32), jnp.ones((1, width - q_width), F32)], axis=1)
    fox_scale = qkv_scale(FOX_HEADS * HEAD_DIM, FOX_WIDTH)
    swa_scale = qkv_scale(SWA_HEADS * HEAD_DIM, SWA_WIDTH)

    h, r = prenorm(xs, vec(g_mix_pre[0]))
    for l in range(depth):
        gates, w_gu = matmul(h, w_gate, l, 3 * d, BF16, 1024, 1024, row_scale=r, act="sigmoid", chunks=1,
                             side=SideCast(w_gate_up, l), name="gate_proj")
        qkv_fox, w_br = matmul(h, w_fox, l, FOX_WIDTH, BF16, 1024, 1024, row_scale=r, col_scale=fox_scale,
                               chunks=1, side=SideCast(w_branch, l), name="fox_proj")
        qkv_swa = matmul(h, w_swa, l, SWA_WIDTH, BF16, 1024, SWA_WIDTH // 2, row_scale=r, col_scale=swa_scale,
                         chunks=1, name="swa_proj")
        cqn, ckvn, k_rot, z_slab, w_o = latent_proj(h, r, w_lat, l, vec(g_q_lora[l]), vec(g_kv_lora[l]),
                                                    cos_t, sin_t, SideCast(w_out, l))
        q_mla = q_up_proj(cqn, w_q, l, cos_t, sin_t)
        kv_mla = matmul(ckvn, w_kv, l, w_kv.shape[2], BF16, 1024, 1024, name="kv_up_proj")
        f_t = fox_decay(z_slab, b_rows[l], batch, seq)
        o_mla = mla_attention(q_mla, kv_mla, k_rot, batch, seq)
        o_swa = swa_attention(qkv_swa, pos_b, pos_row, slopes * LOG2_E, swa_sinks[l].astype(F32) * LOG2_E,
                              batch, seq)
        o_fox = fox_attention(qkv_fox, f_t, batch, seq)
        merged = gated_merge(o_mla, o_swa, o_fox, w_br, 0, gates)
        xs, h2, r2 = proj_norm(merged, w_o, 0, xs, vec(g_mix_post[l]), vec(g_ffn_pre[l]), 512, 1024)
        act, w_dn = ffn_up(h2, r2, w_gu, 0, SideCast(w_down, l))
        g_next = vec(g_mix_pre[l + 1]) if l + 1 < depth else None
        xs, h, r = proj_norm(act, w_dn, 0, xs, vec(g_ffn_post[l]), g_next, 512, 512, single_buffer_a=True)
    return xs.reshape(batch, seq, d)
```
